```python
import math
import jax, jax.numpy as jnp
from jax import lax
import numpy as np

D_MODEL = 1024
BATCH = 8
SEQ = 4096
DEPTH = 1
DEC_BATCH = 8
DEC_SEQ = 32
PAST_LEN = 1024

CHUNK = 64
Q_BLOCK = 128
D_MIX = D_MODEL
A_HEADS = 4
A_HEAD_DIM = 64
A_WIDTH = A_HEADS * 2 * A_HEAD_DIM
B_HEADS = 4
B_KEY_DIM = 128
B_VAL_DIM = 128
B_KEY_WIDTH = B_HEADS * B_KEY_DIM
B_WIDTH = B_HEADS * B_VAL_DIM
IN_SPLITS = (A_WIDTH, A_WIDTH, A_WIDTH, B_KEY_WIDTH, B_KEY_WIDTH, B_WIDTH, B_WIDTH)
IN_COLS = sum(IN_SPLITS)
N_MEM = 256
X_HEADS = 4
X_HEAD_DIM = D_MODEL // X_HEADS
N_EXPERTS = 32
TOP_K = 4
D_EXPERT = D_MODEL
SWIGLU_LIMIT = 7.0
SWIGLU_ALPHA = 1.702
EPS = 1e-6

kernel_name = "hymba_diffattn_hgrn2_moe_streaming_step"

F32 = jnp.float32


def rms_norm(x, g):
    xf = x.astype(F32)
    y = xf * lax.rsqrt(jnp.mean(xf * xf, axis=-1, keepdims=True) + EPS)
    return (y * g.astype(F32)).astype(x.dtype)


def lambda_init(layer):
    return 0.8 - 0.6 * math.exp(-0.3 * layer)


def diff_lambda(lq1, lk1, lq2, lk2, lam0):
    e1 = jnp.exp(jnp.sum(lq1.astype(F32) * lk1.astype(F32)))
    e2 = jnp.exp(jnp.sum(lq2.astype(F32) * lk2.astype(F32)))
    return e1 - e2 + lam0


def mixer_inputs(x, norm_g, w_in, qn_g, kn_g, lb):
    b, t, _ = x.shape
    h = rms_norm(x, norm_g)
    offsets = tuple(int(o) for o in np.cumsum(IN_SPLITS)[:-1])
    qa, ka, va, qb, fb, ib, gb = jnp.split(h @ w_in, offsets, axis=-1)
    q = rms_norm(qa.reshape(b, t, 2 * A_HEADS, A_HEAD_DIM), qn_g)
    k = rms_norm(ka.reshape(b, t, 2 * A_HEADS, A_HEAD_DIM), kn_g)
    v = va.reshape(b, t, A_HEADS, 2 * A_HEAD_DIM)
    f = lb + (1.0 - lb) * jax.nn.sigmoid(fb.astype(F32))
    hq = jax.nn.silu(qb.astype(F32)).reshape(b, t, B_HEADS, B_KEY_DIM)
    hk = (1.0 - f).reshape(b, t, B_HEADS, B_KEY_DIM)
    hlf = jnp.log(f).reshape(b, t, B_HEADS, B_KEY_DIM)
    hv = ib.astype(F32).reshape(b, t, B_HEADS, B_VAL_DIM)
    return (q, k, v), (hq, hk, hv, hlf), gb


def diff_weights(scores, lam):
    p = jax.nn.softmax(scores, axis=-1)
    b, _, nq, nk = p.shape
    p = p.reshape(b, A_HEADS, 2, nq, nk)
    return p[:, :, 0] - lam * p[:, :, 1]


def diff_attn_prompt(q, k, v, lam):
    b, s = q.shape[:2]
    nblk = s // Q_BLOCK
    scale = A_HEAD_DIM ** -0.5
    kf = k.astype(F32)
    key_chunk = jnp.arange(s) // CHUNK
    qblocks = q.reshape(b, nblk, Q_BLOCK, 2 * A_HEADS, A_HEAD_DIM).swapaxes(0, 1)

    def one_block(args):
        qi, bi = args
        q_chunk = (bi * Q_BLOCK + jnp.arange(Q_BLOCK)) // CHUNK
        mask = key_chunk[None, :] <= q_chunk[:, None]
        sc = jnp.einsum('bqhd,bkhd->bhqk', qi.astype(F32), kf) * scale
        sc = jnp.where(mask, sc, -jnp.inf)
        a = diff_weights(sc, lam).astype(v.dtype)
        return jnp.einsum('bhqk,bkhe->bqhe', a, v)

    out = lax.map(one_block, (qblocks, jnp.arange(nblk)))
    return out.swapaxes(0, 1).reshape(b, s, A_HEADS, 2 * A_HEAD_DIM)


def diff_attn_sample(q, k_all, v_all, lam):
    scale = A_HEAD_DIM ** -0.5
    sc = jnp.einsum('bqhd,bkhd->bhqk', q.astype(F32), k_all.astype(F32)) * scale
    a = diff_weights(sc, lam).astype(v_all.dtype)
    return jnp.einsum('bhqk,bkhe->bqhe', a, v_all)


def hgrn_block(S0, q, k, v, logf):
    t = q.shape[1]
    cum = jnp.cumsum(logf, axis=1)
    causal = jnp.tril(jnp.ones((t, t), dtype=bool))[None, :, :, None, None]
    rel = cum[:, :, None] - cum[:, None, :]
    decay = jnp.where(causal, jnp.exp(jnp.minimum(rel, 0.0)), 0.0)
    scores = jnp.einsum('bthk,bshk,btshk->bhts', q, k, decay)
    o = (jnp.einsum('bthk,bhkv->bthv', q * jnp.exp(cum), S0)
         + jnp.einsum('bhts,bshv->bthv', scores, v))
    last = cum[:, -1]
    S_new = (jnp.exp(last)[..., None] * S0
             + jnp.einsum('bshk,bshv->bhkv', k * jnp.exp(last[:, None] - cum), v))
    return S_new, o


def hgrn_prompt(q, k, v, logf):
    b, s = q.shape[:2]
    nblk = s // CHUNK

    def to_blocks(a):
        return a.reshape(b, nblk, CHUNK, *a.shape[2:]).swapaxes(0, 1)

    S0 = jnp.zeros((b, B_HEADS, B_KEY_DIM, B_VAL_DIM), F32)
    S_fin, o = lax.scan(lambda S, xs: hgrn_block(S, *xs), S0,
                        (to_blocks(q), to_blocks(k), to_blocks(v), to_blocks(logf)))
    return S_fin, o.swapaxes(0, 1).reshape(b, s, B_HEADS, B_VAL_DIM)


def mixer_output(oa, ob, gb, subln_g, gn_g, w_out, lam0):
    b, t = oa.shape[:2]
    ya = (rms_norm(oa, subln_g) * (1.0 - lam0)).reshape(b, t, A_WIDTH)
    yb = (rms_norm(ob, gn_g).reshape(b, t, B_WIDTH) * jax.nn.silu(gb.astype(F32))).astype(gb.dtype)
    return jnp.concatenate([ya.astype(gb.dtype), yb], axis=-1) @ w_out


def mem_kv(mem, norm_mem_g, w_mk, w_mv, xk_g):
    b, m, _ = mem.shape
    hm = rms_norm(mem, norm_mem_g)
    k = rms_norm((hm @ w_mk).reshape(b, m, X_HEADS, X_HEAD_DIM), xk_g)
    v = (hm @ w_mv).reshape(b, m, X_HEADS, X_HEAD_DIM)
    return k, v


def cross_attn(x, mk, mv, norm_g, w_xq, xq_g, w_xo):
    b, t, _ = x.shape
    q = rms_norm((rms_norm(x, norm_g) @ w_xq).reshape(b, t, X_HEADS, X_HEAD_DIM), xq_g)
    sc = jnp.einsum('bthd,bmhd->bhtm', q.astype(F32), mk.astype(F32)) * (X_HEAD_DIM ** -0.5)
    p = jax.nn.softmax(sc, axis=-1).astype(mv.dtype)
    o = jnp.einsum('bhtm,bmhd->bthd', p, mv).reshape(b, t, X_HEADS * X_HEAD_DIM)
    return o @ w_xo


def moe(x, norm_g, w_router, b_router, w_gate_up, b_gate_up, w_down, b_down):
    b, t, d = x.shape
    h = rms_norm(x, norm_g).reshape(b * t, d)
    logits = (h @ w_router).astype(F32) + b_router.astype(F32)
    top_val, top_idx = lax.top_k(logits, TOP_K)
    gates = jax.nn.softmax(top_val, axis=-1)
    combine = jnp.sum(jax.nn.one_hot(top_idx, N_EXPERTS, dtype=F32) * gates[..., None], axis=1)
    y = jnp.zeros((b * t, d), F32)
    for e in range(N_EXPERTS):
        gu = h @ w_gate_up[e] + b_gate_up[e]
        gate = jnp.minimum(gu[:, :D_EXPERT], SWIGLU_LIMIT)
        up = jnp.clip(gu[:, D_EXPERT:], -SWIGLU_LIMIT, SWIGLU_LIMIT)
        act = (up + 1.0) * gate * jax.nn.sigmoid(SWIGLU_ALPHA * gate)
        y = y + combine[:, e:e + 1] * (act @ w_down[e] + b_down[e]).astype(F32)
    return y.reshape(b, t, d).astype(x.dtype)


def setup_inputs(seed: int = 0) -> dict:
    key = jax.random.key(seed)
    keys = iter(jax.random.split(key, 48))

    def nrm(shape, scale):
        return jax.random.normal(next(keys), shape, F32) * scale

    def gain(shape):
        return 1.0 + nrm(shape, 0.02)

    L = DEPTH
    d = D_MODEL
    return {
        "x_prompt": nrm((BATCH, SEQ, d), 1.0),
        "x_sample": nrm((DEC_BATCH, DEC_SEQ, d), 1.0),
        "mem_prompt": nrm((BATCH, N_MEM, d), 1.0),
        "cache_attn_k": nrm((L, DEC_BATCH, PAST_LEN, 2 * A_HEADS, A_HEAD_DIM), 1.0),
        "cache_attn_v": nrm((L, DEC_BATCH, PAST_LEN, A_HEADS, 2 * A_HEAD_DIM), 1.0),
        "state_hgrn": nrm((L, DEC_BATCH, B_HEADS, B_KEY_DIM, B_VAL_DIM), 0.5),
        "cache_mem_k": nrm((L, DEC_BATCH, N_MEM, X_HEADS, X_HEAD_DIM), 1.0),
        "cache_mem_v": nrm((L, DEC_BATCH, N_MEM, X_HEADS, X_HEAD_DIM), 1.0),
        "norm_mix_g": gain((L, d)),
        "w_in": nrm((L, d, IN_COLS), d ** -0.5),
        "a_qnorm_g": gain((L, A_HEAD_DIM)),
        "a_knorm_g": gain((L, A_HEAD_DIM)),
        "lambda_q1": nrm((L, A_HEAD_DIM), 0.1),
        "lambda_k1": nrm((L, A_HEAD_DIM), 0.1),
        "lambda_q2": nrm((L, A_HEAD_DIM), 0.1),
        "lambda_k2": nrm((L, A_HEAD_DIM), 0.1),
        "a_subln_g": gain((L, 2 * A_HEAD_DIM)),
        "hgrn_lb_logits": nrm((L + 1, B_KEY_WIDTH), 0.5),
        "b_gnorm_g": gain((L, B_VAL_DIM)),
        "w_out": nrm((L, D_MIX, d), D_MIX ** -0.5),
        "norm_x_g": gain((L, d)),
        "norm_mem_g": gain((L, d)),
        "w_xq": nrm((L, d, X_HEADS * X_HEAD_DIM), d ** -0.5),
        "w_mk": nrm((L, d, X_HEADS * X_HEAD_DIM), d ** -0.5),
        "w_mv": nrm((L, d, X_HEADS * X_HEAD_DIM), d ** -0.5),
        "xq_norm_g": gain((L, X_HEAD_DIM)),
        "xk_norm_g": gain((L, X_HEAD_DIM)),
        "w_xo": nrm((L, X_HEADS * X_HEAD_DIM, d), (X_HEADS * X_HEAD_DIM) ** -0.5),
        "norm_ffn_g": gain((L, d)),
        "w_router": nrm((L, d, N_EXPERTS), d ** -0.5),
        "b_router": nrm((L, N_EXPERTS), 0.01),
        "w_gate_up": nrm((L, N_EXPERTS, d, 2 * D_EXPERT), d ** -0.5),
        "b_gate_up": nrm((L, N_EXPERTS, 2 * D_EXPERT), 0.01),
        "w_down": nrm((L, N_EXPERTS, D_EXPERT, d), D_EXPERT ** -0.5),
        "b_down": nrm((L, N_EXPERTS, d), 0.01),
    }


def reference(x_prompt, x_sample, mem_prompt, cache_attn_k, cache_attn_v, state_hgrn,
              cache_mem_k, cache_mem_v, norm_mix_g, w_in, a_qnorm_g, a_knorm_g,
              lambda_q1, lambda_k1, lambda_q2, lambda_k2, a_subln_g, hgrn_lb_logits,
              b_gnorm_g, w_out, norm_x_g, norm_mem_g, w_xq, w_mk, w_mv, xq_norm_g,
              xk_norm_g, w_xo, norm_ffn_g, w_router, b_router, w_gate_up, b_gate_up,
              w_down, b_down):
    lb_all = jnp.cumsum(jax.nn.softmax(hgrn_lb_logits.astype(F32), axis=0), axis=0)
    xp, xs = x_prompt, x_sample
    kp_l, vp_l, sp_l, mkp_l, mvp_l, ks_l, vs_l, ss_l = [], [], [], [], [], [], [], []
    for l in range(DEPTH):
        lam0 = lambda_init(l)
        lam = diff_lambda(lambda_q1[l], lambda_k1[l], lambda_q2[l], lambda_k2[l], lam0)

        (q, k, v), (hq, hk, hv, hlf), gb = mixer_inputs(
            xp, norm_mix_g[l], w_in[l], a_qnorm_g[l], a_knorm_g[l], lb_all[l])
        oa = diff_attn_prompt(q, k, v, lam)
        s_p, ob = hgrn_prompt(hq, hk, hv, hlf)
        xp = xp + mixer_output(oa, ob, gb, a_subln_g[l], b_gnorm_g[l], w_out[l], lam0)
        mk_p, mv_p = mem_kv(mem_prompt, norm_mem_g[l], w_mk[l], w_mv[l], xk_norm_g[l])
        xp = xp + cross_attn(xp, mk_p, mv_p, norm_x_g[l], w_xq[l], xq_norm_g[l], w_xo[l])
        xp = xp + moe(xp, norm_ffn_g[l], w_router[l], b_router[l], w_gate_up[l],
                      b_gate_up[l], w_down[l], b_down[l])
        kp_l.append(k); vp_l.append(v); sp_l.append(s_p); mkp_l.append(mk_p); mvp_l.append(mv_p)

        (q, k, v), (hq, hk, hv, hlf), gb = mixer_inputs(
            xs, norm_mix_g[l], w_in[l], a_qnorm_g[l], a_knorm_g[l], lb_all[l])
        k_all = jnp.concatenate([cache_attn_k[l].astype(k.dtype), k], axis=1)
        v_all = jnp.concatenate([cache_attn_v[l].astype(v.dtype), v], axis=1)
        oa = diff_attn_sample(q, k_all, v_all, lam)
        s_s, ob = hgrn_block(state_hgrn[l].astype(F32), hq, hk, hv, hlf)
        xs = xs + mixer_output(oa, ob, gb, a_subln_g[l], b_gnorm_g[l], w_out[l], lam0)
        xs = xs + cross_attn(xs, cache_mem_k[l], cache_mem_v[l], norm_x_g[l], w_xq[l],
                             xq_norm_g[l], w_xo[l])
        xs = xs + moe(xs, norm_ffn_g[l], w_router[l], b_router[l], w_gate_up[l],
                      b_gate_up[l], w_down[l], b_down[l])
        ks_l.append(k); vs_l.append(v); ss_l.append(s_s)

    return (xp, xs,
            jnp.stack(kp_l), jnp.stack(vp_l), jnp.stack(sp_l), jnp.stack(mkp_l), jnp.stack(mvp_l),
            jnp.stack(ks_l), jnp.stack(vs_l), jnp.stack(ss_l))
```

```python
import functools
import math

import jax
import jax.numpy as jnp
from jax import lax
from jax.experimental import pallas as pl
from jax.experimental.pallas import tpu as pltpu

F32 = jnp.float32
BF16 = jnp.bfloat16
I32 = jnp.int32

EPS = 1e-6
CHUNK = 64
A_HEADS = 4
A_HEAD_DIM = 64
A_WIDTH = A_HEADS * 2 * A_HEAD_DIM
B_HEADS = 4
B_DIM = 128
X_HEADS = 4
N_EXPERTS = 32
TOP_K = 4
SWIGLU_LIMIT = 7.0
SWIGLU_ALPHA = 1.702

LANES = 128
VMEM_LIMIT = 56 * 1024 * 1024


def _cparams(sem):
    return pltpu.CompilerParams(dimension_semantics=sem, vmem_limit_bytes=VMEM_LIMIT)


def _const_spec(shape):
    zeros = (0,) * len(shape)
    return pl.BlockSpec(shape, lambda *_: zeros)


def _rms(x, g):
    return x * lax.rsqrt(jnp.mean(x * x, axis=-1, keepdims=True) + EPS) * g


def _dot(a, b):
    return jnp.dot(a, b, preferred_element_type=F32)


def _dot_nt(a, b):
    return lax.dot_general(a, b, (((1,), (1,)), ((), ())), preferred_element_type=F32)


def _dot_tn(a, b):
    return lax.dot_general(a, b, (((0,), (0,)), ((), ())), preferred_element_type=F32)


def _sigmoid(x):
    return 1.0 / (1.0 + jnp.exp(-x))


def _mixer_in_kernel(x_ref, g_ref, w_ref, gq_ref, gk_ref, lbl_ref, seg_ref,
                     qs_ref, kb_ref, vb_ref, k32_ref, v32_ref,
                     hq_ref, hk_ref, hv_ref, lf_ref, gt_ref, *, layer):
    hb = _rms(x_ref[...], g_ref[...]).astype(BF16)

    def proj(j):
        return _dot(hb, w_ref[:, j * A_WIDTH:(j + 1) * A_WIDTH])

    seg = seg_ref[...]

    def head_norm(a, g):
        sq = a * a
        hi = sq.astype(BF16)
        lo = (sq - hi.astype(F32)).astype(BF16)
        ss = _dot(hi, seg) + _dot(lo, seg)
        return a * lax.rsqrt(ss * (1.0 / A_HEAD_DIM) + EPS) * g

    q = head_norm(proj(0), gq_ref[...])
    qs_ref[...] = (q * (A_HEAD_DIM ** -0.5)).astype(BF16)
    k = head_norm(proj(1), gk_ref[...])
    k32_ref[...] = k
    kb_ref[...] = k.astype(BF16)
    v = proj(2)
    v32_ref[...] = v
    vb_ref[...] = v.astype(BF16)

    lbl = lbl_ref[...]
    e = jnp.exp(lbl - jnp.max(lbl, axis=0, keepdims=True))
    lb = jnp.sum(e[:layer + 1], axis=0, keepdims=True) / jnp.sum(e, axis=0, keepdims=True)

    qb = proj(3)
    hq_ref[...] = (qb * _sigmoid(qb)).astype(BF16)
    f = lb + (1.0 - lb) * _sigmoid(proj(4))
    hk_ref[...] = (1.0 - f).astype(BF16)
    lf_ref[...] = jnp.log(f)
    hv_ref[...] = proj(5).astype(BF16)
    gb = proj(6)
    gt_ref[...] = (gb * _sigmoid(gb)).astype(BF16)


def _mixer_in(x2d, norm_g, w_in_bf, gq, gk, lb_logits, layer, tm):
    t, d = x2d.shape
    w = A_WIDTH
    seg = (jnp.arange(w)[:, None] // A_HEAD_DIM == jnp.arange(w)[None, :] // A_HEAD_DIM).astype(BF16)
    row = lambda i: (i, 0)
    out_dt = [BF16, BF16, BF16, F32, F32, BF16, BF16, BF16, F32, BF16]
    return pl.pallas_call(
        functools.partial(_mixer_in_kernel, layer=layer),
        out_shape=[jax.ShapeDtypeStruct((t, w), dt) for dt in out_dt],
        grid=(t // tm,),
        in_specs=[pl.BlockSpec((tm, d), row), _const_spec((1, d)), _const_spec(w_in_bf.shape),
                  _const_spec((1, w)), _const_spec((1, w)), _const_spec(lb_logits.shape),
                  _const_spec((w, w))],
        out_specs=[pl.BlockSpec((tm, w), row) for _ in out_dt],
        compiler_params=_cparams(("parallel",)),
        name="mixer_in",
    )(x2d, norm_g.reshape(1, d), w_in_bf,
      jnp.tile(gq, 2 * A_HEADS).reshape(1, w), jnp.tile(gk, 2 * A_HEADS).reshape(1, w),
      lb_logits, seg)


def _lambda(lam_ref, lam0):
    lv = lam_ref[...]
    e1 = jnp.exp(jnp.sum(lv[0:1] * lv[1:2], axis=-1, keepdims=True))
    e2 = jnp.exp(jnp.sum(lv[2:3] * lv[3:4], axis=-1, keepdims=True))
    return e1 - e2 + lam0


def _split_maps(k):
    first = lax.broadcasted_iota(I32, k.shape, 1) < A_HEAD_DIM
    zero = jnp.zeros_like(k)
    return jnp.where(first, k, zero), jnp.where(first, zero, k)


def _attn_prompt_kernel(lam_ref, q_ref, k_ref, v_ref, o_ref, *, tq, lam0):
    qi = pl.program_id(2)
    q = q_ref[0]

    def step(j, carry, masked):
        k = k_ref[0, pl.ds(pl.multiple_of(j * tq, tq), tq), :]
        v = v_ref[0, pl.ds(pl.multiple_of(j * tq, tq), tq), :]
        new = []
        for km, (m, l, acc) in zip(_split_maps(k), carry):
            s = _dot_nt(q, km)
            if masked:
                rc = lax.broadcasted_iota(I32, s.shape, 0) // CHUNK
                cc = lax.broadcasted_iota(I32, s.shape, 1) // CHUNK
                s = jnp.where(cc <= rc, s, -jnp.inf)
            m_new = jnp.maximum(m, jnp.max(s, axis=-1, keepdims=True))
            alpha = jnp.exp(m - m_new)
            p = jnp.exp(s - m_new)
            l_new = alpha * l + jnp.sum(p, axis=-1, keepdims=True)
            acc_new = alpha * acc + _dot(p.astype(BF16), v)
            new.append((m_new, l_new, acc_new))
        return tuple(new)

    init = tuple((jnp.full((tq, 1), -jnp.inf, F32), jnp.zeros((tq, 1), F32),
                  jnp.zeros((tq, LANES), F32)) for _ in range(2))
    carry = lax.fori_loop(0, qi, lambda j, c: step(j, c, False), init)
    (_, l1, a1), (_, l2, a2) = step(qi, carry, True)
    o_ref[0] = (a1 / l1 - _lambda(lam_ref, lam0) * (a2 / l2)).astype(o_ref.dtype)


def _attn_prompt(lam_vec, qs, kb, vb, lam0, tq):
    b, s, w = qs.shape
    blk = lambda bi, hi, qi: (bi, qi, hi)
    full = lambda bi, hi, qi: (bi, 0, hi)
    return pl.pallas_call(
        functools.partial(_attn_prompt_kernel, tq=tq, lam0=lam0),
        out_shape=jax.ShapeDtypeStruct((b, s, w), F32),
        grid=(b, A_HEADS, s // tq),
        in_specs=[_const_spec(lam_vec.shape), pl.BlockSpec((1, tq, LANES), blk),
                  pl.BlockSpec((1, s, LANES), full), pl.BlockSpec((1, s, LANES), full)],
        out_specs=pl.BlockSpec((1, tq, LANES), blk),
        compiler_params=_cparams(("parallel", "parallel", "arbitrary")),
        name="attn_prompt",
    )(lam_vec, qs, kb, vb)


def _attn_sample_kernel(lam_ref, q_ref, kc_ref, vc_ref, kn_ref, vn_ref, o_ref, *, lam0):
    q = q_ref[0]
    vc = vc_ref[0]
    vn = vn_ref[0]
    outs = []
    for kcm, knm in zip(_split_maps(kc_ref[0]), _split_maps(kn_ref[0])):
        sc = _dot_nt(q, kcm)
        sn = _dot_nt(q, knm)
        m = jnp.maximum(jnp.max(sc, axis=-1, keepdims=True), jnp.max(sn, axis=-1, keepdims=True))
        pc = jnp.exp(sc - m)
        pn = jnp.exp(sn - m)
        l = jnp.sum(pc, axis=-1, keepdims=True) + jnp.sum(pn, axis=-1, keepdims=True)
        outs.append((_dot(pc.astype(BF16), vc) + _dot(pn.astype(BF16), vn)) / l)
    o_ref[0] = (outs[0] - _lambda(lam_ref, lam0) * outs[1]).astype(o_ref.dtype)


def _attn_sample(lam_vec, qs, kc, vc, kn, vn, lam0):
    b, t, w = qs.shape
    past = kc.shape[1]
    blk = lambda bi, hi: (bi, 0, hi)
    return pl.pallas_call(
        functools.partial(_attn_sample_kernel, lam0=lam0),
        out_shape=jax.ShapeDtypeStruct((b, t, w), F32),
        grid=(b, A_HEADS),
        in_specs=[_const_spec(lam_vec.shape), pl.BlockSpec((1, t, LANES), blk),
                  pl.BlockSpec((1, past, LANES), blk), pl.BlockSpec((1, past, LANES), blk),
                  pl.BlockSpec((1, t, LANES), blk), pl.BlockSpec((1, t, LANES), blk)],
        out_specs=pl.BlockSpec((1, t, LANES), blk),
        compiler_params=_cparams(("parallel", "parallel")),
        name="attn_sample",
    )(lam_vec, qs, kc, vc, kn, vn)


def _cumsum_rows(x):
    c = x.shape[0]
    row = lax.broadcasted_iota(I32, x.shape, 0)
    d = 1
    while d < c:
        x = x + jnp.where(row >= d, pltpu.roll(x, d, axis=0), 0.0)
        d *= 2
    return x


def _block_ref_rows(cum, m):
    c, n = cum.shape
    if m >= 8:
        r = cum.reshape(c // (2 * m), 2 * m, n)[:, m - 1:m, :]
        return jnp.broadcast_to(r, (c // (2 * m), 2 * m, n)).reshape(c, n)
    pos = lax.broadcasted_iota(I32, cum.shape, 0) % (2 * m)
    out = cum
    for p in range(2 * m):
        shift = p - (m - 1)
        if shift != 0:
            out = jnp.where(pos == p, pltpu.roll(cum, shift % c, axis=0), out)
    return out


def _hgrn_kernel(hq_ref, hk_ref, hv_ref, lf_ref, s0_ref, o_ref, st_ref, stt_ref, *, c):
    ci = pl.program_id(1)

    @pl.when(ci == 0)
    def _():
        for h in range(B_HEADS):
            stt_ref[h] = s0_ref[0, h].T

    row = lax.broadcasted_iota(I32, (c, B_DIM), 0)
    r2 = lax.broadcasted_iota(I32, (c, c), 0)
    c2 = lax.broadcasted_iota(I32, (c, c), 1)
    for h in range(B_HEADS):
        sl = slice(h * B_DIM, (h + 1) * B_DIM)
        hq = hq_ref[0, :, sl].astype(F32)
        hk = hk_ref[0, :, sl].astype(F32)
        hv = hv_ref[0, :, sl]
        cum = _cumsum_rows(lf_ref[0, :, sl])
        last = cum[c - 1:c, :]

        a = jnp.zeros((c, c), F32)
        m = c // 2
        while m >= 1:
            ref = _block_ref_rows(cum, m)
            upper = (row % (2 * m)) >= m
            qd = jnp.where(upper, hq * jnp.exp(jnp.minimum(cum - ref, 0.0)), 0.0)
            kd = jnp.where(upper, 0.0, hk * jnp.exp(jnp.minimum(ref - cum, 0.0)))
            p = _dot_nt(qd.astype(BF16), kd.astype(BF16))
            a = a + jnp.where((r2 // (2 * m)) == (c2 // (2 * m)), p, 0.0)
            m //= 2

        stt = stt_ref[h]
        o = _dot_nt((hq * jnp.exp(cum)).astype(BF16), stt.astype(BF16))
        o = o + _dot(a.astype(BF16), hv)
        o = o + jnp.sum(hq * hk, axis=-1, keepdims=True) * hv.astype(F32)
        o_ref[0, :, sl] = o.astype(o_ref.dtype)

        kl = (hk * jnp.exp(last - cum)).astype(BF16)
        stt_ref[h] = stt * jnp.exp(last) + _dot_tn(hv, kl)

    @pl.when(ci == pl.num_programs(1) - 1)
    def _():
        for h in range(B_HEADS):
            st_ref[0, h] = stt_ref[h].T


def _hgrn(hq, hk, hv, lf, s0, c):
    b, s, w = hq.shape
    blk = lambda bi, ci: (bi, ci, 0)
    sblk = lambda bi, ci: (bi, 0, 0, 0)
    return pl.pallas_call(
        functools.partial(_hgrn_kernel, c=c),
        out_shape=[jax.ShapeDtypeStruct((b, s, w), F32),
                   jax.ShapeDtypeStruct((b, B_HEADS, B_DIM, B_DIM), F32)],
        grid=(b, s // c),
        in_specs=[pl.BlockSpec((1, c, w), blk)] * 4 + [pl.BlockSpec((1, B_HEADS, B_DIM, B_DIM), sblk)],
        out_specs=[pl.BlockSpec((1, c, w), blk), pl.BlockSpec((1, B_HEADS, B_DIM, B_DIM), sblk)],
        scratch_shapes=[pltpu.VMEM((B_HEADS, B_DIM, B_DIM), F32)],
        compiler_params=_cparams(("parallel", "arbitrary")),
        name="hgrn",
    )(hq, hk, hv, lf, s0)


def _mem_kv_kernel(m_ref, g_ref, wk_ref, wv_ref, gk_ref, k_ref, v_ref):
    hm = _rms(m_ref[...], g_ref[...]).astype(BF16)
    k = _dot(hm, wk_ref[...])
    gk = gk_ref[...]
    hd = k.shape[-1] // X_HEADS
    for h in range(X_HEADS):
        sl = slice(h * hd, (h + 1) * hd)
        k_ref[:, sl] = _rms(k[:, sl], gk)
    v_ref[...] = _dot(hm, wv_ref[...])


def _mem_kv(mem2d, norm_g, w_mk_bf, w_mv_bf, xk_g, tm):
    t, d = mem2d.shape
    row = lambda i: (i, 0)
    return pl.pallas_call(
        _mem_kv_kernel,
        out_shape=[jax.ShapeDtypeStruct((t, d), F32)] * 2,
        grid=(t // tm,),
        in_specs=[pl.BlockSpec((tm, d), row), _const_spec((1, d)), _const_spec(w_mk_bf.shape),
                  _const_spec(w_mv_bf.shape), _const_spec((1, xk_g.shape[-1]))],
        out_specs=[pl.BlockSpec((tm, d), row)] * 2,
        compiler_params=_cparams(("parallel",)),
        name="mem_kv",
    )(mem2d, norm_g.reshape(1, d), w_mk_bf, w_mv_bf, xk_g.reshape(1, -1))


def _post_kernel(x_ref, oa_ref, ob_ref, gt_ref, sg_ref, gg_ref, wo_ref,
                 nx_ref, wq_ref, xq_ref, mk_ref, mv_ref, wxo_ref,
                 nf_ref, wr_ref, br_ref,
                 x2_ref, h2_ref, idx_ref, gate_ref, *, lam0):
    ya = []
    yb = []
    for h in range(A_HEADS):
        sl = slice(h * LANES, (h + 1) * LANES)
        ya.append((_rms(oa_ref[:, sl], sg_ref[...]) * (1.0 - lam0)).astype(BF16))
        yb.append((_rms(ob_ref[:, sl], gg_ref[...]) * gt_ref[:, sl].astype(F32)).astype(BF16))
    y = jnp.concatenate(ya + yb, axis=-1)
    x1 = x_ref[...] + _dot(y, wo_ref[...])

    q = _dot(_rms(x1, nx_ref[...]).astype(BF16), wq_ref[...])
    hd = q.shape[-1] // X_HEADS
    os_ = []
    for h in range(X_HEADS):
        sl = slice(h * hd, (h + 1) * hd)
        qh = _rms(q[:, sl], xq_ref[...]).astype(BF16)
        s = _dot_nt(qh, mk_ref[0, :, sl]) * (hd ** -0.5)
        p = jnp.exp(s - jnp.max(s, axis=-1, keepdims=True))
        p = p / jnp.sum(p, axis=-1, keepdims=True)
        os_.append(_dot(p.astype(BF16), mv_ref[0, :, sl]).astype(BF16))
    x2 = x1 + _dot(jnp.concatenate(os_, axis=-1), wxo_ref[...])
    x2_ref[...] = x2

    h2 = _rms(x2, nf_ref[...])
    h2_ref[...] = h2
    logits = jnp.dot(h2, wr_ref[...], preferred_element_type=F32,
                     precision=lax.Precision.HIGHEST) + br_ref[...]
    col = lax.broadcasted_iota(I32, logits.shape, 1)
    lane = lax.broadcasted_iota(I32, idx_ref.shape, 1)
    idx_out = jnp.zeros(idx_ref.shape, I32)
    val_out = jnp.full(gate_ref.shape, -jnp.inf, F32)
    top = None
    for k in range(TOP_K):
        mx = jnp.max(logits, axis=-1, keepdims=True)
        ix = jnp.min(jnp.where(logits == mx, col, N_EXPERTS), axis=-1, keepdims=True)
        top = mx if top is None else top
        idx_out = jnp.where(lane == k, ix, idx_out)
        val_out = jnp.where(lane == k, mx, val_out)
        logits = jnp.where(col == ix, -jnp.inf, logits)
    e = jnp.exp(val_out - top)
    idx_ref[...] = idx_out
    gate_ref[...] = e / jnp.sum(e, axis=-1, keepdims=True)


def _post(x2d, oa, ob, gt, subln_g, gn_g, w_out_bf, norm_x_g, w_xq_bf, xq_g, mk_bf, mv_bf,
          w_xo_bf, norm_ffn_g, w_router, b_router, lam0, tm, rows_per_batch):
    t, d = x2d.shape
    row = lambda i: (i, 0)
    tiles_per_batch = rows_per_batch // tm
    mem = lambda i: (i // tiles_per_batch, 0, 0)
    n_mem = mk_bf.shape[1]
    vec = lambda a: a.reshape(1, -1)
    return pl.pallas_call(
        functools.partial(_post_kernel, lam0=lam0),
        out_shape=[jax.ShapeDtypeStruct((t, d), F32), jax.ShapeDtypeStruct((t, d), F32),
                   jax.ShapeDtypeStruct((t, LANES), I32), jax.ShapeDtypeStruct((t, LANES), F32)],
        grid=(t // tm,),
        in_specs=[pl.BlockSpec((tm, d), row), pl.BlockSpec((tm, A_WIDTH), row),
                  pl.BlockSpec((tm, A_WIDTH), row), pl.BlockSpec((tm, A_WIDTH), row),
                  _const_spec((1, LANES)), _const_spec((1, LANES)), _const_spec(w_out_bf.shape),
                  _const_spec((1, d)), _const_spec(w_xq_bf.shape), _const_spec((1, xq_g.shape[-1])),
                  pl.BlockSpec((1, n_mem, d), mem), pl.BlockSpec((1, n_mem, d), mem),
                  _const_spec(w_xo_bf.shape),
                  _const_spec((1, d)), _const_spec(w_router.shape), _const_spec((1, N_EXPERTS))],
        out_specs=[pl.BlockSpec((tm, d), row), pl.BlockSpec((tm, d), row),
                   pl.BlockSpec((tm, LANES), row), pl.BlockSpec((tm, LANES), row)],
        compiler_params=_cparams(("parallel",)),
        name="post",
    )(x2d, oa, ob, gt, vec(subln_g), vec(gn_g), w_out_bf, vec(norm_x_g), w_xq_bf, vec(xq_g),
      mk_bf, mv_bf, w_xo_bf, vec(norm_ffn_g), w_router, vec(b_router))


def _moe_expert_kernel(te_ref, tv_ref, tf_ref, src_ref, h_hbm, wgu_ref, bgu_ref, wd_ref, bd_ref,
                       y_ref, xbuf, wgu_bf, wd_bf, sem, *, tm):
    i = pl.program_id(0)

    @pl.when(tv_ref[i] == 0)
    def _():
        y_ref[...] = jnp.zeros_like(y_ref)

    @pl.when(tv_ref[i] == 1)
    def _():
        def issue(r, _):
            pltpu.make_async_copy(h_hbm.at[pl.ds(src_ref[0, 0, r], 1)],
                                  xbuf.at[pl.ds(r, 1)], sem).start()
            return 0
        lax.fori_loop(0, tm, issue, 0)

        @pl.when(tf_ref[i] == 1)
        def _():
            wgu_bf[...] = wgu_ref[0].astype(BF16)
            wd_bf[...] = wd_ref[0].astype(BF16)

        pltpu.make_async_copy(h_hbm.at[pl.ds(0, tm)], xbuf, sem).wait()
        de = wd_bf.shape[0]
        gu = _dot(xbuf[...].astype(BF16), wgu_bf[...]) + bgu_ref[0]
        gate = jnp.minimum(gu[:, :de], SWIGLU_LIMIT)
        up = jnp.clip(gu[:, de:], -SWIGLU_LIMIT, SWIGLU_LIMIT)
        act = (up + 1.0) * gate * _sigmoid(SWIGLU_ALPHA * gate)
        y_ref[...] = _dot(act.astype(BF16), wd_bf[...]) + bd_ref[0]


def _moe_expert(tile_expert, tile_valid, tile_first, src3, h2, w_gate_up, b_gate_up, w_down, b_down, tm):
    nt = tile_expert.shape[0]
    d = h2.shape[1]
    ne, _, dgu = w_gate_up.shape
    de = w_down.shape[1]
    grid_spec = pltpu.PrefetchScalarGridSpec(
        num_scalar_prefetch=3,
        grid=(nt,),
        in_specs=[pl.BlockSpec((1, 1, tm), lambda i, te, tv, tf: (i, 0, 0), memory_space=pltpu.SMEM),
                  pl.BlockSpec(memory_space=pl.ANY),
                  pl.BlockSpec((1, d, dgu), lambda i, te, tv, tf: (te[i], 0, 0)),
                  pl.BlockSpec((1, 1, dgu), lambda i, te, tv, tf: (te[i], 0, 0)),
                  pl.BlockSpec((1, de, d), lambda i, te, tv, tf: (te[i], 0, 0)),
                  pl.BlockSpec((1, 1, d), lambda i, te, tv, tf: (te[i], 0, 0))],
        out_specs=pl.BlockSpec((tm, d), lambda i, te, tv, tf: (i, 0)),
        scratch_shapes=[pltpu.VMEM((tm, d), F32), pltpu.VMEM((d, dgu), BF16),
                        pltpu.VMEM((de, d), BF16), pltpu.SemaphoreType.DMA],
    )
    return pl.pallas_call(
        functools.partial(_moe_expert_kernel, tm=tm),
        out_shape=jax.ShapeDtypeStruct((nt * tm, d), F32),
        grid_spec=grid_spec,
        compiler_params=_cparams(("arbitrary",)),
        name="moe_expert",
    )(tile_expert, tile_valid, tile_first, src3, h2, w_gate_up,
      b_gate_up.reshape(ne, 1, dgu), w_down, b_down.reshape(ne, 1, d))


def _moe_combine_kernel(dst_ref, x_ref, gate_ref, y_hbm, o_ref, ybuf, sem, *, tm):
    def issue(r, _):
        for k in range(TOP_K):
            pltpu.make_async_copy(y_hbm.at[pl.ds(dst_ref[0, 0, r * TOP_K + k], 1)],
                                  ybuf.at[k, pl.ds(r, 1)], sem).start()
        return 0
    lax.fori_loop(0, tm, issue, 0)
    for k in range(TOP_K):
        pltpu.make_async_copy(y_hbm.at[pl.ds(0, tm)], ybuf.at[k], sem).wait()
    g = gate_ref[...]
    acc = x_ref[...]
    for k in range(TOP_K):
        acc = acc + g[:, k:k + 1] * ybuf[k]
    o_ref[...] = acc


def _moe_combine(dst3, x2, gate, y, tm):
    t, d = x2.shape
    row = lambda i: (i, 0)
    return pl.pallas_call(
        functools.partial(_moe_combine_kernel, tm=tm),
        out_shape=jax.ShapeDtypeStruct((t, d), F32),
        grid=(t // tm,),
        in_specs=[pl.BlockSpec((1, 1, tm * TOP_K), lambda i: (i, 0, 0), memory_space=pltpu.SMEM),
                  pl.BlockSpec((tm, d), row), pl.BlockSpec((tm, LANES), row),
                  pl.BlockSpec(memory_space=pl.ANY)],
        out_specs=pl.BlockSpec((tm, d), row),
        scratch_shapes=[pltpu.VMEM((TOP_K, tm, d), F32), pltpu.SemaphoreType.DMA],
        compiler_params=_cparams(("arbitrary",)),
        name="moe_combine",
    )(dst3, x2, gate, y)


def _moe(x2, h2, idx, gate, w_gate_up, b_gate_up, w_down, b_down, tm_e, tm_c):
    t = x2.shape[0]
    picks = idx[:, :TOP_K].reshape(-1)
    onehot = (picks[:, None] == jnp.arange(N_EXPERTS, dtype=I32)[None, :]).astype(I32)
    rank = jnp.sum((jnp.cumsum(onehot, axis=0) - 1) * onehot, axis=1)
    counts = jnp.sum(onehot, axis=0)
    padded = ((counts + tm_e - 1) // tm_e) * tm_e
    ends = jnp.cumsum(padded)
    starts = ends - padded
    dest = starts[picks] + rank
    nt = (t * TOP_K) // tm_e + N_EXPERTS
    src = jnp.zeros((nt * tm_e,), I32).at[dest].set(jnp.arange(t * TOP_K, dtype=I32) // TOP_K)
    tile_start = jnp.arange(nt, dtype=I32) * tm_e
    tile_expert = jnp.minimum(jnp.sum((tile_start[:, None] >= ends[None, :]).astype(I32), axis=1),
                              N_EXPERTS - 1).astype(I32)
    tile_valid = (tile_start < ends[-1]).astype(I32)
    tile_first = jnp.concatenate([jnp.ones((1,), I32),
                                  (tile_expert[1:] != tile_expert[:-1]).astype(I32)])
    y = _moe_expert(tile_expert, tile_valid, tile_first, src.reshape(nt, 1, tm_e), h2,
                    w_gate_up, b_gate_up, w_down, b_down, tm_e)
    return _moe_combine(dest.astype(I32).reshape(t // tm_c, 1, tm_c * TOP_K), x2, gate, y, tm_c)


def _pick(n, pref):
    return pref if n % pref == 0 else n


def _group(x, mk_bf, mv_bf, attn_fn, s0, l, p, bf, lam0):
    b, t, d = x.shape
    x2d = x.reshape(b * t, d)
    (qs, kb, vb, k32, v32, hq, hk, hv, lf, gt) = _mixer_in(
        x2d, p["norm_mix_g"][l], bf["w_in"][l], p["a_qnorm_g"][l], p["a_knorm_g"][l],
        p["hgrn_lb_logits"], l, _pick(b * t, 512))
    r3 = lambda a: a.reshape(b, t, -1)
    oa = attn_fn(r3(qs), r3(kb), r3(vb))
    ob, s_new = _hgrn(r3(hq), r3(hk), r3(hv), r3(lf), s0, _pick(t, 64))
    x2, h2, idx, gate = _post(
        x2d, oa.reshape(b * t, -1), ob.reshape(b * t, -1), gt, p["a_subln_g"][l], p["b_gnorm_g"][l],
        bf["w_out"][l], p["norm_x_g"][l], bf["w_xq"][l], p["xq_norm_g"][l], mk_bf, mv_bf,
        bf["w_xo"][l], p["norm_ffn_g"][l], p["w_router"][l], p["b_router"][l], lam0,
        _pick(t, 256), t)
    y = _moe(x2, h2, idx, gate, p["w_gate_up"][l], p["b_gate_up"][l], p["w_down"][l], p["b_down"][l],
             _pick(b * t * TOP_K, 256) if b * t >= 4096 else 64, _pick(b * t, 128))
    return (y.reshape(b, t, d), k32.reshape(b, t, 2 * A_HEADS, A_HEAD_DIM),
            v32.reshape(b, t, A_HEADS, 2 * A_HEAD_DIM), s_new)


def kernel(x_prompt, x_sample, mem_prompt, cache_attn_k, cache_attn_v, state_hgrn, cache_mem_k, cache_mem_v, norm_mix_g, w_in, a_qnorm_g, a_knorm_g, lambda_q1, lambda_k1, lambda_q2, lambda_k2, a_subln_g, hgrn_lb_logits, b_gnorm_g, w_out, norm_x_g, norm_mem_g, w_xq, w_mk, w_mv, xq_norm_g, xk_norm_g, w_xo, norm_ffn_g, w_router, b_router, w_gate_up, b_gate_up, w_down, b_down):
    p = dict(norm_mix_g=norm_mix_g, a_qnorm_g=a_qnorm_g, a_knorm_g=a_knorm_g, a_subln_g=a_subln_g,
             hgrn_lb_logits=hgrn_lb_logits, b_gnorm_g=b_gnorm_g, norm_x_g=norm_x_g,
             xq_norm_g=xq_norm_g, norm_ffn_g=norm_ffn_g, w_router=w_router, b_router=b_router,
             w_gate_up=w_gate_up, b_gate_up=b_gate_up, w_down=w_down, b_down=b_down)
    bf = dict(w_in=w_in.astype(BF16), w_out=w_out.astype(BF16), w_xq=w_xq.astype(BF16),
              w_xo=w_xo.astype(BF16))
    depth = w_in.shape[0]
    b, s, d = x_prompt.shape
    db, ds, _ = x_sample.shape
    n_mem = mem_prompt.shape[1]
    xp, xs = x_prompt, x_sample
    outs = [[] for _ in range(8)]
    for l in range(depth):
        lam0 = 0.8 - 0.6 * math.exp(-0.3 * l)
        lam_vec = jnp.stack([lambda_q1[l], lambda_k1[l], lambda_q2[l], lambda_k2[l]])

        mk, mv = _mem_kv(mem_prompt.reshape(b * n_mem, d), norm_mem_g[l], w_mk[l].astype(BF16),
                         w_mv[l].astype(BF16), xk_norm_g[l], _pick(b * n_mem, 512))
        attn_p = lambda q, k, v: _attn_prompt(lam_vec, q, k, v, lam0, _pick(s, 256))
        s0 = jnp.zeros((b, B_HEADS, B_DIM, B_DIM), F32)
        xp, kp, vp, sp = _group(xp, mk.reshape(b, n_mem, d).astype(BF16),
                                mv.reshape(b, n_mem, d).astype(BF16), attn_p, s0, l, p, bf, lam0)

        kc = cache_attn_k[l].reshape(db, -1, A_WIDTH).astype(BF16)
        vc = cache_attn_v[l].reshape(db, -1, A_WIDTH).astype(BF16)
        attn_s = lambda q, k, v: _attn_sample(lam_vec, q, kc, vc, k, v, lam0)
        xs, ks, vs, ss = _group(xs, cache_mem_k[l].reshape(db, n_mem, d).astype(BF16),
                                cache_mem_v[l].reshape(db, n_mem, d).astype(BF16), attn_s,
                                state_hgrn[l], l, p, bf, lam0)
        for lst, val in zip(outs, (kp, vp, sp, mk.reshape(b, n_mem, X_HEADS, -1),
                                   mv.reshape(b, n_mem, X_HEADS, -1), ks, vs, ss)):
            lst.append(val)
    return (xp, xs) + tuple(jnp.stack(o) for o in outs)
```

```python
import functools
import math

import jax
import jax.numpy as jnp
from jax import lax
from jax.experimental import pallas as pl
from jax.experimental.pallas import tpu as pltpu

F32 = jnp.float32
BF16 = jnp.bfloat16
I32 = jnp.int32

EPS = 1e-6
CHUNK = 64
A_HEADS = 4
A_HEAD_DIM = 64
A_WIDTH = A_HEADS * 2 * A_HEAD_DIM
B_HEADS = 4
B_DIM = 128
X_HEADS = 4
N_EXPERTS = 32
TOP_K = 4
SWIGLU_LIMIT = 7.0
SWIGLU_ALPHA = 1.702

LANES = 128
VMEM_LIMIT = 56 * 1024 * 1024


def _cparams(sem):
    return pltpu.CompilerParams(dimension_semantics=sem, vmem_limit_bytes=VMEM_LIMIT)


def _const_spec(shape):
    zeros = (0,) * len(shape)
    return pl.BlockSpec(shape, lambda *_: zeros)


def _rms(x, g):
    return x * lax.rsqrt(jnp.mean(x * x, axis=-1, keepdims=True) + EPS) * g


def _dot(a, b):
    return jnp.dot(a, b, preferred_element_type=F32)


def _dot_nt(a, b):
    return lax.dot_general(a, b, (((1,), (1,)), ((), ())), preferred_element_type=F32)


def _dot_tn(a, b):
    return lax.dot_general(a, b, (((0,), (0,)), ((), ())), preferred_element_type=F32)


def _sigmoid(x):
    return 1.0 / (1.0 + jnp.exp(-x))


def _mixer_in_kernel(x_ref, g_ref, w_ref, wvt_ref, gq_ref, gk_ref, lbl_ref, seg_ref,
                     qs_ref, kb_ref, vb_ref, k32_ref, v32_ref,
                     hq_ref, hk_ref, hv_ref, lf_ref, gt_ref, *, layer, tk):
    hb = _rms(x_ref[...], g_ref[...]).astype(BF16)

    def proj(j):
        return _dot(hb, w_ref[:, j * A_WIDTH:(j + 1) * A_WIDTH])

    seg = seg_ref[...]

    def head_norm(a, g):
        sq = a * a
        hi = sq.astype(BF16)
        lo = (sq - hi.astype(F32)).astype(BF16)
        ss = _dot(hi, seg) + _dot(lo, seg)
        return a * lax.rsqrt(ss * (1.0 / A_HEAD_DIM) + EPS) * g

    q = head_norm(proj(0), gq_ref[...])
    qs_ref[...] = (q * (A_HEAD_DIM ** -0.5 * math.log2(math.e))).astype(BF16)
    k = head_norm(proj(1), gk_ref[...])
    k32_ref[...] = k
    kb_ref[...] = k.astype(BF16)
    v = proj(2)
    v32_ref[...] = v
    if tk is None:
        vb_ref[...] = v.astype(BF16)
    else:
        vt = _dot_nt(wvt_ref[...], hb).astype(BF16)
        for c in range(vb_ref.shape[0]):
            vb_ref[c] = vt[:, c * tk:(c + 1) * tk]

    lbl = lbl_ref[...]
    e = jnp.exp(lbl - jnp.max(lbl, axis=0, keepdims=True))
    lb = jnp.sum(e[:layer + 1], axis=0, keepdims=True) / jnp.sum(e, axis=0, keepdims=True)

    qb = proj(3)
    hq_ref[...] = (qb * _sigmoid(qb)).astype(BF16)
    f = lb + (1.0 - lb) * _sigmoid(proj(4))
    hk_ref[...] = (1.0 - f).astype(BF16)
    lf_ref[...] = jnp.log(f)
    hv_ref[...] = proj(5).astype(BF16)
    gb = proj(6)
    gt_ref[...] = (gb * _sigmoid(gb)).astype(BF16)


def _mixer_in(x2d, norm_g, w_in_bf, gq, gk, lb_logits, layer, tm, tk):
    t, d = x2d.shape
    w = A_WIDTH
    seg = (jnp.arange(w)[:, None] // A_HEAD_DIM == jnp.arange(w)[None, :] // A_HEAD_DIM).astype(BF16)
    row = lambda i: (i, 0)
    out_dt = [BF16, BF16, BF16, F32, F32, BF16, BF16, BF16, F32, BF16]
    out_shape = [jax.ShapeDtypeStruct((t, w), dt) for dt in out_dt]
    out_specs = [pl.BlockSpec((tm, w), row) for _ in out_dt]
    if tk is not None:
        out_shape[2] = jax.ShapeDtypeStruct((t // tk, w, tk), BF16)
        out_specs[2] = pl.BlockSpec((tm // tk, w, tk), lambda i: (i, 0, 0))
    wvt = w_in_bf[:, 2 * w:3 * w].T
    return pl.pallas_call(
        functools.partial(_mixer_in_kernel, layer=layer, tk=tk),
        out_shape=out_shape,
        grid=(t // tm,),
        in_specs=[pl.BlockSpec((tm, d), row), _const_spec((1, d)), _const_spec(w_in_bf.shape),
                  _const_spec(wvt.shape), _const_spec((1, w)), _const_spec((1, w)),
                  _const_spec(lb_logits.shape), _const_spec((w, w))],
        out_specs=out_specs,
        compiler_params=_cparams(("parallel",)),
        name="mixer_in",
    )(x2d, norm_g.reshape(1, d), w_in_bf, wvt,
      jnp.tile(gq, 2 * A_HEADS).reshape(1, w), jnp.tile(gk, 2 * A_HEADS).reshape(1, w),
      lb_logits, seg)


def _lambda(lam_ref, lam0):
    lv = lam_ref[...]
    e1 = jnp.exp(jnp.sum(lv[0:1] * lv[1:2], axis=-1, keepdims=True))
    e2 = jnp.exp(jnp.sum(lv[2:3] * lv[3:4], axis=-1, keepdims=True))
    return e1 - e2 + lam0


def _split_maps(k):
    first = lax.broadcasted_iota(I32, k.shape, 1) < A_HEAD_DIM
    zero = jnp.zeros_like(k)
    return jnp.where(first, k, zero), jnp.where(first, zero, k)


def _attn_prompt_kernel(lam_ref, q_ref, k_ref, vt_ref, o_ref,
                        sa_ref, sb_ref, pa_ref, pb_ref, acc_ref, *, tq, tk, lam0):
    assert tq == 2 * tk
    qi = pl.program_id(2)
    q = q_ref[0]
    s_bufs = (sa_ref, sb_ref)
    p_bufs = (pa_ref, pb_ref)

    def phase(j, par, mx_cur, ml, masked, prefetch):
        s_ref, p_ref = s_bufs[par], p_bufs[par]
        sn_ref, pp_ref = s_bufs[1 - par], p_bufs[1 - par]
        vt = vt_ref[jnp.maximum(j - 1, 0)]
        mx_next = []
        if prefetch:
            k = k_ref[0, pl.ds(pl.multiple_of((j + 1) * tk, tk), tk), :]
            for mp, km in enumerate(_split_maps(k)):
                s = _dot_nt(km, q)
                sn_ref[mp] = s
                mx_next.append(jnp.max(s, axis=0, keepdims=True))
        new_ml = []
        for mp in range(2):
            m, l = ml[mp]
            pv = _dot(vt, pp_ref[mp])
            s = s_ref[mp]
            s_max = mx_cur[mp]
            if masked:
                kc = (j * tk + lax.broadcasted_iota(I32, s.shape, 0)) // CHUNK
                qc = (qi * tq + lax.broadcasted_iota(I32, s.shape, 1)) // CHUNK
                s = jnp.where(kc <= qc, s, -jnp.inf)
                s_max = jnp.max(s, axis=0, keepdims=True)
            m_new = jnp.maximum(m, s_max)
            alpha = jnp.exp2(m - m_new)
            p = jnp.exp2(s - m_new)
            p_ref[mp] = p.astype(BF16)
            acc_ref[mp] = alpha * (acc_ref[mp] + pv)
            new_ml.append((m_new, alpha * l + jnp.sum(p, axis=0, keepdims=True)))
        return tuple(mx_next), tuple(new_ml)

    k0 = k_ref[0, pl.ds(0, tk), :]
    mx = []
    for mp, km in enumerate(_split_maps(k0)):
        s = _dot_nt(km, q)
        sa_ref[mp] = s
        mx.append(jnp.max(s, axis=0, keepdims=True))
    pb_ref[...] = jnp.zeros_like(pb_ref)
    acc_ref[...] = jnp.zeros_like(acc_ref)
    ml = tuple((jnp.full((1, tq), -jnp.inf, F32), jnp.zeros((1, tq), F32)) for _ in range(2))

    def pair(t, carry):
        mx_a, ml = carry
        mx_b, ml = phase(2 * t, 0, mx_a, ml, False, True)
        mx_a, ml = phase(2 * t + 1, 1, mx_b, ml, False, True)
        return mx_a, ml

    mx_a, ml = lax.fori_loop(0, qi, pair, (tuple(mx), ml))
    mx_b, ml = phase(2 * qi, 0, mx_a, ml, True, True)
    _, ml = phase(2 * qi + 1, 1, mx_b, ml, True, False)
    vt = vt_ref[2 * qi + 1]
    (_, l1), (_, l2) = ml
    o1 = (acc_ref[0] + _dot(vt, pb_ref[0])) / l1
    o2 = (acc_ref[1] + _dot(vt, pb_ref[1])) / l2
    o_ref[0] = (o1 - _lambda(lam_ref, lam0) * o2).T.astype(o_ref.dtype)


def _attn_prompt(lam_vec, qs, kb, vt, lam0, tq, tk):
    b, s, w = qs.shape
    nkb = s // tk
    blk = lambda bi, hi, qi: (bi, qi, hi)
    return pl.pallas_call(
        functools.partial(_attn_prompt_kernel, tq=tq, tk=tk, lam0=lam0),
        out_shape=jax.ShapeDtypeStruct((b, s, w), F32),
        grid=(b, A_HEADS, s // tq),
        in_specs=[_const_spec(lam_vec.shape), pl.BlockSpec((1, tq, LANES), blk),
                  pl.BlockSpec((1, s, LANES), lambda bi, hi, qi: (bi, 0, hi)),
                  pl.BlockSpec((nkb, LANES, tk), lambda bi, hi, qi: (bi, hi, 0))],
        out_specs=pl.BlockSpec((1, tq, LANES), blk),
        scratch_shapes=[pltpu.VMEM((2, tk, tq), F32), pltpu.VMEM((2, tk, tq), F32),
                        pltpu.VMEM((2, tk, tq), BF16), pltpu.VMEM((2, tk, tq), BF16),
                        pltpu.VMEM((2, LANES, tq), F32)],
        compiler_params=_cparams(("parallel", "parallel", "arbitrary")),
        name="attn_prompt",
    )(lam_vec, qs, kb, vt)


def _attn_sample_kernel(lam_ref, q_ref, kc_ref, vc_ref, kn_ref, vn_ref, o_ref, *, lam0):
    q = q_ref[0]
    vc = vc_ref[0]
    vn = vn_ref[0]
    outs = []
    for kcm, knm in zip(_split_maps(kc_ref[0]), _split_maps(kn_ref[0])):
        sc = _dot_nt(q, kcm)
        sn = _dot_nt(q, knm)
        m = jnp.maximum(jnp.max(sc, axis=-1, keepdims=True), jnp.max(sn, axis=-1, keepdims=True))
        pc = jnp.exp2(sc - m)
        pn = jnp.exp2(sn - m)
        l = jnp.sum(pc, axis=-1, keepdims=True) + jnp.sum(pn, axis=-1, keepdims=True)
        outs.append((_dot(pc.astype(BF16), vc) + _dot(pn.astype(BF16), vn)) / l)
    o_ref[0] = (outs[0] - _lambda(lam_ref, lam0) * outs[1]).astype(o_ref.dtype)


def _attn_sample(lam_vec, qs, kc, vc, kn, vn, lam0):
    b, t, w = qs.shape
    past = kc.shape[1]
    blk = lambda bi, hi: (bi, 0, hi)
    return pl.pallas_call(
        functools.partial(_attn_sample_kernel, lam0=lam0),
        out_shape=jax.ShapeDtypeStruct((b, t, w), F32),
        grid=(b, A_HEADS),
        in_specs=[_const_spec(lam_vec.shape), pl.BlockSpec((1, t, LANES), blk),
                  pl.BlockSpec((1, past, LANES), blk), pl.BlockSpec((1, past, LANES), blk),
                  pl.BlockSpec((1, t, LANES), blk), pl.BlockSpec((1, t, LANES), blk)],
        out_specs=pl.BlockSpec((1, t, LANES), blk),
        compiler_params=_cparams(("parallel", "parallel")),
        name="attn_sample",
    )(lam_vec, qs, kc, vc, kn, vn)


def _cumsum_rows(x):
    c = x.shape[0]
    row = lax.broadcasted_iota(I32, x.shape, 0)
    d = 1
    while d < c:
        x = x + jnp.where(row >= d, pltpu.roll(x, d, axis=0), 0.0)
        d *= 2
    return x


def _block_ref_rows(cum, m):
    c, n = cum.shape
    if m >= 8:
        r = cum.reshape(c // (2 * m), 2 * m, n)[:, m - 1:m, :]
        return jnp.broadcast_to(r, (c // (2 * m), 2 * m, n)).reshape(c, n)
    pos = lax.broadcasted_iota(I32, cum.shape, 0) % (2 * m)
    out = cum
    for p in range(2 * m):
        shift = p - (m - 1)
        if shift != 0:
            out = jnp.where(pos == p, pltpu.roll(cum, shift % c, axis=0), out)
    return out


def _hgrn_kernel(hq_ref, hk_ref, hv_ref, lf_ref, s0_ref, o_ref, st_ref, stt_ref, *, c):
    ci = pl.program_id(1)

    @pl.when(ci == 0)
    def _():
        for h in range(B_HEADS):
            stt_ref[h] = s0_ref[0, h].T

    row = lax.broadcasted_iota(I32, (c, B_DIM), 0)
    r2 = lax.broadcasted_iota(I32, (c, c), 0)
    c2 = lax.broadcasted_iota(I32, (c, c), 1)
    for h in range(B_HEADS):
        sl = slice(h * B_DIM, (h + 1) * B_DIM)
        hq = hq_ref[0, :, sl].astype(F32)
        hk = hk_ref[0, :, sl].astype(F32)
        hv = hv_ref[0, :, sl]
        cum = _cumsum_rows(lf_ref[0, :, sl])
        last = cum[c - 1:c, :]

        a = jnp.zeros((c, c), F32)
        m = c // 2
        while m >= 1:
            ref = _block_ref_rows(cum, m)
            upper = (row % (2 * m)) >= m
            qd = jnp.where(upper, hq * jnp.exp(jnp.minimum(cum - ref, 0.0)), 0.0)
            kd = jnp.where(upper, 0.0, hk * jnp.exp(jnp.minimum(ref - cum, 0.0)))
            p = _dot_nt(qd.astype(BF16), kd.astype(BF16))
            a = a + jnp.where((r2 // (2 * m)) == (c2 // (2 * m)), p, 0.0)
            m //= 2

        stt = stt_ref[h]
        o = _dot_nt((hq * jnp.exp(cum)).astype(BF16), stt.astype(BF16))
        o = o + _dot(a.astype(BF16), hv)
        o = o + jnp.sum(hq * hk, axis=-1, keepdims=True) * hv.astype(F32)
        o_ref[0, :, sl] = o.astype(o_ref.dtype)

        kl = (hk * jnp.exp(last - cum)).astype(BF16)
        stt_ref[h] = stt * jnp.exp(last) + _dot_tn(hv, kl)

    @pl.when(ci == pl.num_programs(1) - 1)
    def _():
        for h in range(B_HEADS):
            st_ref[0, h] = stt_ref[h].T


def _hgrn(hq, hk, hv, lf, s0, c):
    b, s, w = hq.shape
    blk = lambda bi, ci: (bi, ci, 0)
    sblk = lambda bi, ci: (bi, 0, 0, 0)
    return pl.pallas_call(
        functools.partial(_hgrn_kernel, c=c),
        out_shape=[jax.ShapeDtypeStruct((b, s, w), F32),
                   jax.ShapeDtypeStruct((b, B_HEADS, B_DIM, B_DIM), F32)],
        grid=(b, s // c),
        in_specs=[pl.BlockSpec((1, c, w), blk)] * 4 + [pl.BlockSpec((1, B_HEADS, B_DIM, B_DIM), sblk)],
        out_specs=[pl.BlockSpec((1, c, w), blk), pl.BlockSpec((1, B_HEADS, B_DIM, B_DIM), sblk)],
        scratch_shapes=[pltpu.VMEM((B_HEADS, B_DIM, B_DIM), F32)],
        compiler_params=_cparams(("parallel", "arbitrary")),
        name="hgrn",
    )(hq, hk, hv, lf, s0)


def _mem_kv_kernel(m_ref, g_ref, wk_ref, wv_ref, gk_ref, k_ref, v_ref):
    hm = _rms(m_ref[...], g_ref[...]).astype(BF16)
    k = _dot(hm, wk_ref[...])
    gk = gk_ref[...]
    hd = k.shape[-1] // X_HEADS
    for h in range(X_HEADS):
        sl = slice(h * hd, (h + 1) * hd)
        k_ref[:, sl] = _rms(k[:, sl], gk)
    v_ref[...] = _dot(hm, wv_ref[...])


def _mem_kv(mem2d, norm_g, w_mk_bf, w_mv_bf, xk_g, tm):
    t, d = mem2d.shape
    row = lambda i: (i, 0)
    return pl.pallas_call(
        _mem_kv_kernel,
        out_shape=[jax.ShapeDtypeStruct((t, d), F32)] * 2,
        grid=(t // tm,),
        in_specs=[pl.BlockSpec((tm, d), row), _const_spec((1, d)), _const_spec(w_mk_bf.shape),
                  _const_spec(w_mv_bf.shape), _const_spec((1, xk_g.shape[-1]))],
        out_specs=[pl.BlockSpec((tm, d), row)] * 2,
        compiler_params=_cparams(("parallel",)),
        name="mem_kv",
    )(mem2d, norm_g.reshape(1, d), w_mk_bf, w_mv_bf, xk_g.reshape(1, -1))


def _post_kernel(x_ref, oa_ref, ob_ref, gt_ref, sg_ref, gg_ref, wo_ref,
                 nx_ref, wq_ref, xq_ref, mk_ref, mv_ref, wxo_ref,
                 nf_ref, wr_ref, br_ref,
                 x2_ref, h2_ref, idx_ref, gate_ref, *, lam0):
    ya = []
    yb = []
    for h in range(A_HEADS):
        sl = slice(h * LANES, (h + 1) * LANES)
        ya.append((_rms(oa_ref[:, sl], sg_ref[...]) * (1.0 - lam0)).astype(BF16))
        yb.append((_rms(ob_ref[:, sl], gg_ref[...]) * gt_ref[:, sl].astype(F32)).astype(BF16))
    y = jnp.concatenate(ya + yb, axis=-1)
    x1 = x_ref[...] + _dot(y, wo_ref[...])

    q = _dot(_rms(x1, nx_ref[...]).astype(BF16), wq_ref[...])
    hd = q.shape[-1] // X_HEADS
    os_ = []
    for h in range(X_HEADS):
        sl = slice(h * hd, (h + 1) * hd)
        qh = _rms(q[:, sl], xq_ref[...]).astype(BF16)
        s = _dot_nt(qh, mk_ref[0, :, sl]) * (hd ** -0.5)
        p = jnp.exp(s - jnp.max(s, axis=-1, keepdims=True))
        p = p / jnp.sum(p, axis=-1, keepdims=True)
        os_.append(_dot(p.astype(BF16), mv_ref[0, :, sl]).astype(BF16))
    x2 = x1 + _dot(jnp.concatenate(os_, axis=-1), wxo_ref[...])
    x2_ref[...] = x2

    h2 = _rms(x2, nf_ref[...])
    h2_ref[...] = h2
    logits = jnp.dot(h2, wr_ref[...], preferred_element_type=F32,
                     precision=lax.Precision.HIGHEST) + br_ref[...]
    col = lax.broadcasted_iota(I32, logits.shape, 1)
    lane = lax.broadcasted_iota(I32, idx_ref.shape, 1)
    idx_out = jnp.zeros(idx_ref.shape, I32)
    val_out = jnp.full(gate_ref.shape, -jnp.inf, F32)
    top = None
    for k in range(TOP_K):
        mx = jnp.max(logits, axis=-1, keepdims=True)
        ix = jnp.min(jnp.where(logits == mx, col, N_EXPERTS), axis=-1, keepdims=True)
        top = mx if top is None else top
        idx_out = jnp.where(lane == k, ix, idx_out)
        val_out = jnp.where(lane == k, mx, val_out)
        logits = jnp.where(col == ix, -jnp.inf, logits)
    e = jnp.exp(val_out - top)
    idx_ref[...] = idx_out
    gate_ref[...] = e / jnp.sum(e, axis=-1, keepdims=True)


def _post(x2d, oa, ob, gt, subln_g, gn_g, w_out_bf, norm_x_g, w_xq_bf, xq_g, mk_bf, mv_bf,
          w_xo_bf, norm_ffn_g, w_router, b_router, lam0, tm, rows_per_batch):
    t, d = x2d.shape
    row = lambda i: (i, 0)
    tiles_per_batch = rows_per_batch // tm
    mem = lambda i: (i // tiles_per_batch, 0, 0)
    n_mem = mk_bf.shape[1]
    vec = lambda a: a.reshape(1, -1)
    return pl.pallas_call(
        functools.partial(_post_kernel, lam0=lam0),
        out_shape=[jax.ShapeDtypeStruct((t, d), F32), jax.ShapeDtypeStruct((t, d), F32),
                   jax.ShapeDtypeStruct((t, LANES), I32), jax.ShapeDtypeStruct((t, LANES), F32)],
        grid=(t // tm,),
        in_specs=[pl.BlockSpec((tm, d), row), pl.BlockSpec((tm, A_WIDTH), row),
                  pl.BlockSpec((tm, A_WIDTH), row), pl.BlockSpec((tm, A_WIDTH), row),
                  _const_spec((1, LANES)), _const_spec((1, LANES)), _const_spec(w_out_bf.shape),
                  _const_spec((1, d)), _const_spec(w_xq_bf.shape), _const_spec((1, xq_g.shape[-1])),
                  pl.BlockSpec((1, n_mem, d), mem), pl.BlockSpec((1, n_mem, d), mem),
                  _const_spec(w_xo_bf.shape),
                  _const_spec((1, d)), _const_spec(w_router.shape), _const_spec((1, N_EXPERTS))],
        out_specs=[pl.BlockSpec((tm, d), row), pl.BlockSpec((tm, d), row),
                   pl.BlockSpec((tm, LANES), row), pl.BlockSpec((tm, LANES), row)],
        compiler_params=_cparams(("parallel",)),
        name="post",
    )(x2d, oa, ob, gt, vec(subln_g), vec(gn_g), w_out_bf, vec(norm_x_g), w_xq_bf, vec(xq_g),
      mk_bf, mv_bf, w_xo_bf, vec(norm_ffn_g), w_router, vec(b_router))


def _moe_expert_kernel(te_ref, tf_ref, src0_ref, srcn_ref, h_hbm, wgu_ref, bgu_ref, wd_ref, bd_ref,
                       y_ref, xbuf, wgu_bf, wd_bf, sem, *, tm):
    i = pl.program_id(0)
    last = pl.num_programs(0) - 1
    slot = i % 2

    def row_copy(src_ref, r, s):
        return pltpu.make_async_copy(h_hbm.at[pl.ds(src_ref[0, 0, r], 1)],
                                     xbuf.at[s, pl.ds(r, 1)], sem.at[s])

    def wait_tile(s):
        pltpu.make_async_copy(h_hbm.at[pl.ds(0, tm)], xbuf.at[s], sem.at[s]).wait()

    @pl.when(i == 0)
    def _():
        def issue(r, _):
            row_copy(src0_ref, r, 0).start()
            return 0
        lax.fori_loop(0, tm, issue, 0)

    @pl.when(tf_ref[i] == 1)
    def _():
        wgu_bf[...] = wgu_ref[0].astype(BF16)
        wd_bf[...] = wd_ref[0].astype(BF16)

    wait_tile(slot)
    for r in range(tm):
        row_copy(srcn_ref, r, 1 - slot).start()

    de = wd_bf.shape[0]
    gu = _dot(xbuf[slot].astype(BF16), wgu_bf[...]) + bgu_ref[0]
    gate = jnp.minimum(gu[:, :de], SWIGLU_LIMIT)
    up = jnp.clip(gu[:, de:], -SWIGLU_LIMIT, SWIGLU_LIMIT)
    act = (up + 1.0) * gate * _sigmoid(SWIGLU_ALPHA * gate)
    y_ref[...] = _dot(act.astype(BF16), wd_bf[...]) + bd_ref[0]

    @pl.when(i == last)
    def _():
        wait_tile(1 - slot)


def _moe_expert(tile_expert, tile_first, src3, h2, w_gate_up, b_gate_up, w_down, b_down, tm):
    nt = tile_expert.shape[0]
    d = h2.shape[1]
    ne, _, dgu = w_gate_up.shape
    de = w_down.shape[1]
    grid_spec = pltpu.PrefetchScalarGridSpec(
        num_scalar_prefetch=2,
        grid=(nt,),
        in_specs=[pl.BlockSpec((1, 1, tm), lambda i, te, tf: (0, 0, 0), memory_space=pltpu.SMEM),
                  pl.BlockSpec((1, 1, tm), lambda i, te, tf: (jnp.minimum(i + 1, nt - 1), 0, 0),
                               memory_space=pltpu.SMEM),
                  pl.BlockSpec(memory_space=pl.ANY),
                  pl.BlockSpec((1, d, dgu), lambda i, te, tf: (te[i], 0, 0)),
                  pl.BlockSpec((1, 1, dgu), lambda i, te, tf: (te[i], 0, 0)),
                  pl.BlockSpec((1, de, d), lambda i, te, tf: (te[i], 0, 0)),
                  pl.BlockSpec((1, 1, d), lambda i, te, tf: (te[i], 0, 0))],
        out_specs=pl.BlockSpec((tm, d), lambda i, te, tf: (i, 0)),
        scratch_shapes=[pltpu.VMEM((2, tm, d), F32), pltpu.VMEM((d, dgu), BF16),
                        pltpu.VMEM((de, d), BF16), pltpu.SemaphoreType.DMA((2,))],
    )
    return pl.pallas_call(
        functools.partial(_moe_expert_kernel, tm=tm),
        out_shape=jax.ShapeDtypeStruct((nt * tm, d), F32),
        grid_spec=grid_spec,
        compiler_params=_cparams(("arbitrary",)),
        name="moe_expert",
    )(tile_expert, tile_first, src3, src3, h2, w_gate_up,
      b_gate_up.reshape(ne, 1, dgu), w_down, b_down.reshape(ne, 1, d))


def _moe_combine_kernel(dst_ref, x_ref, gate_ref, y_hbm, o_ref, ybuf, sem, *, tm):
    def issue(r, _):
        for k in range(TOP_K):
            pltpu.make_async_copy(y_hbm.at[pl.ds(dst_ref[0, 0, r * TOP_K + k], 1)],
                                  ybuf.at[k, pl.ds(r, 1)], sem).start()
        return 0
    lax.fori_loop(0, tm, issue, 0, unroll=8)
    for k in range(TOP_K):
        pltpu.make_async_copy(y_hbm.at[pl.ds(0, tm)], ybuf.at[k], sem).wait()
    g = gate_ref[...]
    acc = x_ref[...]
    for k in range(TOP_K):
        acc = acc + g[:, k:k + 1] * ybuf[k]
    o_ref[...] = acc


def _moe_combine(dst3, x2, gate, y, tm):
    t, d = x2.shape
    row = lambda i: (i, 0)
    return pl.pallas_call(
        functools.partial(_moe_combine_kernel, tm=tm),
        out_shape=jax.ShapeDtypeStruct((t, d), F32),
        grid=(t // tm,),
        in_specs=[pl.BlockSpec((1, 1, tm * TOP_K), lambda i: (i, 0, 0), memory_space=pltpu.SMEM),
                  pl.BlockSpec((tm, d), row), pl.BlockSpec((tm, LANES), row),
                  pl.BlockSpec(memory_space=pl.ANY)],
        out_specs=pl.BlockSpec((tm, d), row),
        scratch_shapes=[pltpu.VMEM((TOP_K, tm, d), F32), pltpu.SemaphoreType.DMA],
        compiler_params=_cparams(("arbitrary",)),
        name="moe_combine",
    )(dst3, x2, gate, y)


def _moe(x2, h2, idx, gate, w_gate_up, b_gate_up, w_down, b_down, tm_e, tm_c):
    t = x2.shape[0]
    picks = idx[:, :TOP_K].reshape(-1)
    onehot = (picks[:, None] == jnp.arange(N_EXPERTS, dtype=I32)[None, :]).astype(I32)
    rank = jnp.sum((jnp.cumsum(onehot, axis=0) - 1) * onehot, axis=1)
    counts = jnp.sum(onehot, axis=0)
    padded = ((counts + tm_e - 1) // tm_e) * tm_e
    ends = jnp.cumsum(padded)
    starts = ends - padded
    dest = starts[picks] + rank
    n = t * TOP_K
    nt = n // tm_e + N_EXPERTS
    slot_e = jnp.repeat(jnp.arange(N_EXPERTS, dtype=I32), tm_e)
    slot_j = jnp.tile(jnp.arange(tm_e, dtype=I32), N_EXPERTS)
    pad_keys = jnp.where(slot_j < (padded - counts)[slot_e], 2 * slot_e + 1, 2 * N_EXPERTS)
    order = jnp.argsort(jnp.concatenate([2 * picks, pad_keys]), stable=True).astype(I32)
    src = jnp.where(order < n, order // TOP_K, 0)
    tile_start = jnp.arange(nt, dtype=I32) * tm_e
    tile_expert = jnp.minimum(jnp.sum((tile_start[:, None] >= ends[None, :]).astype(I32), axis=1),
                              N_EXPERTS - 1).astype(I32)
    tile_first = jnp.concatenate([jnp.ones((1,), I32),
                                  (tile_expert[1:] != tile_expert[:-1]).astype(I32)])
    y = _moe_expert(tile_expert, tile_first, src.reshape(nt, 1, tm_e), h2,
                    w_gate_up, b_gate_up, w_down, b_down, tm_e)
    return _moe_combine(dest.astype(I32).reshape(t // tm_c, 1, tm_c * TOP_K), x2, gate, y, tm_c)


def _pick(n, pref):
    return pref if n % pref == 0 else n


def _group(x, mk_bf, mv_bf, attn_fn, attn_tk, s0, l, p, bf, lam0):
    b, t, d = x.shape
    x2d = x.reshape(b * t, d)
    (qs, kb, vb, k32, v32, hq, hk, hv, lf, gt) = _mixer_in(
        x2d, p["norm_mix_g"][l], bf["w_in"][l], p["a_qnorm_g"][l], p["a_knorm_g"][l],
        p["hgrn_lb_logits"], l, _pick(b * t, 512), attn_tk)
    r3 = lambda a: a.reshape(b, t, -1)
    oa = attn_fn(r3(qs), r3(kb), vb if attn_tk else r3(vb))
    ob, s_new = _hgrn(r3(hq), r3(hk), r3(hv), r3(lf), s0, _pick(t, 64))
    x2, h2, idx, gate = _post(
        x2d, oa.reshape(b * t, -1), ob.reshape(b * t, -1), gt, p["a_subln_g"][l], p["b_gnorm_g"][l],
        bf["w_out"][l], p["norm_x_g"][l], bf["w_xq"][l], p["xq_norm_g"][l], mk_bf, mv_bf,
        bf["w_xo"][l], p["norm_ffn_g"][l], p["w_router"][l], p["b_router"][l], lam0,
        _pick(t, 256), t)
    y = _moe(x2, h2, idx, gate, p["w_gate_up"][l], p["b_gate_up"][l], p["w_down"][l], p["b_down"][l],
             _pick(b * t * TOP_K, 256) if b * t >= 4096 else 64, _pick(b * t, 128))
    return (y.reshape(b, t, d), k32.reshape(b, t, 2 * A_HEADS, A_HEAD_DIM),
            v32.reshape(b, t, A_HEADS, 2 * A_HEAD_DIM), s_new)


def kernel(x_prompt, x_sample, mem_prompt, cache_attn_k, cache_attn_v, state_hgrn, cache_mem_k, cache_mem_v, norm_mix_g, w_in, a_qnorm_g, a_knorm_g, lambda_q1, lambda_k1, lambda_q2, lambda_k2, a_subln_g, hgrn_lb_logits, b_gnorm_g, w_out, norm_x_g, norm_mem_g, w_xq, w_mk, w_mv, xq_norm_g, xk_norm_g, w_xo, norm_ffn_g, w_router, b_router, w_gate_up, b_gate_up, w_down, b_down):
    p = dict(norm_mix_g=norm_mix_g, a_qnorm_g=a_qnorm_g, a_knorm_g=a_knorm_g, a_subln_g=a_subln_g,
             hgrn_lb_logits=hgrn_lb_logits, b_gnorm_g=b_gnorm_g, norm_x_g=norm_x_g,
             xq_norm_g=xq_norm_g, norm_ffn_g=norm_ffn_g, w_router=w_router, b_router=b_router,
             w_gate_up=w_gate_up, b_gate_up=b_gate_up, w_down=w_down, b_down=b_down)
    bf = dict(w_in=w_in.astype(BF16), w_out=w_out.astype(BF16), w_xq=w_xq.astype(BF16),
              w_xo=w_xo.astype(BF16))
    depth = w_in.shape[0]
    b, s, d = x_prompt.shape
    db, ds, _ = x_sample.shape
    n_mem = mem_prompt.shape[1]
    xp, xs = x_prompt, x_sample
    outs = [[] for _ in range(8)]
    for l in range(depth):
        lam0 = 0.8 - 0.6 * math.exp(-0.3 * l)
        lam_vec = jnp.stack([lambda_q1[l], lambda_k1[l], lambda_q2[l], lambda_k2[l]])

        mk, mv = _mem_kv(mem_prompt.reshape(b * n_mem, d), norm_mem_g[l], w_mk[l].astype(BF16),
                         w_mv[l].astype(BF16), xk_norm_g[l], _pick(b * n_mem, 512))
        tk = 2 * LANES
        attn_p = lambda q, k, vt: _attn_prompt(lam_vec, q, k, vt, lam0, 2 * tk, tk)
        s0 = jnp.zeros((b, B_HEADS, B_DIM, B_DIM), F32)
        xp, kp, vp, sp = _group(xp, mk.reshape(b, n_mem, d).astype(BF16),
                                mv.reshape(b, n_mem, d).astype(BF16), attn_p, tk, s0, l, p, bf, lam0)

        kc = cache_attn_k[l].reshape(db, -1, A_WIDTH).astype(BF16)
        vc = cache_attn_v[l].reshape(db, -1, A_WIDTH).astype(BF16)
        attn_s = lambda q, k, v: _attn_sample(lam_vec, q, kc, vc, k, v, lam0)
        xs, ks, vs, ss = _group(xs, cache_mem_k[l].reshape(db, n_mem, d).astype(BF16),
                                cache_mem_v[l].reshape(db, n_mem, d).astype(BF16), attn_s, None,
                                state_hgrn[l], l, p, bf, lam0)
        for lst, val in zip(outs, (kp, vp, sp, mk.reshape(b, n_mem, X_HEADS, -1),
                                   mv.reshape(b, n_mem, X_HEADS, -1), ks, vs, ss)):
            lst.append(val)
    return (xp, xs) + tuple(jnp.stack(o) for o in outs)
```

```python
import functools
import math

import jax
import jax.numpy as jnp
from jax import lax
from jax.experimental import pallas as pl
from jax.experimental.pallas import tpu as pltpu

F32 = jnp.float32
BF16 = jnp.bfloat16
I32 = jnp.int32

EPS = 1e-6
CHUNK = 64
A_HEADS = 4
A_HEAD_DIM = 64
A_WIDTH = A_HEADS * 2 * A_HEAD_DIM
B_HEADS = 4
B_DIM = 128
X_HEADS = 4
N_EXPERTS = 32
TOP_K = 4
SWIGLU_LIMIT = 7.0
SWIGLU_ALPHA = 1.702

LANES = 128
ROW_BLOCK = 8
VMEM_LIMIT = 56 * 1024 * 1024


def _cparams(sem):
    return pltpu.CompilerParams(dimension_semantics=sem, vmem_limit_bytes=VMEM_LIMIT)


def _const_spec(shape):
    zeros = (0,) * len(shape)
    return pl.BlockSpec(shape, lambda *_: zeros)


def _rms(x, g):
    return x * lax.rsqrt(jnp.mean(x * x, axis=-1, keepdims=True) + EPS) * g


def _dot(a, b):
    return jnp.dot(a, b, preferred_element_type=F32)


def _dot_nt(a, b):
    return lax.dot_general(a, b, (((1,), (1,)), ((), ())), preferred_element_type=F32)


def _dot_tn(a, b):
    return lax.dot_general(a, b, (((0,), (0,)), ((), ())), preferred_element_type=F32)


def _sigmoid(x):
    return 1.0 / (1.0 + jnp.exp(-x))


def _mixer_in_kernel(x_ref, g_ref, w_ref, wvt_ref, gq_ref, gk_ref, lbl_ref, seg_ref,
                     qs_ref, kb_ref, vb_ref, k32_ref, v32_ref,
                     hq_ref, hk_ref, hv_ref, lf_ref, gt_ref, *, layer, tk):
    hb = _rms(x_ref[...], g_ref[...]).astype(BF16)

    def proj(j):
        return _dot(hb, w_ref[:, j * A_WIDTH:(j + 1) * A_WIDTH])

    seg = seg_ref[...]

    def head_norm(a, g):
        sq = a * a
        hi = sq.astype(BF16)
        lo = (sq - hi.astype(F32)).astype(BF16)
        ss = _dot(hi, seg) + _dot(lo, seg)
        return a * lax.rsqrt(ss * (1.0 / A_HEAD_DIM) + EPS) * g

    q = head_norm(proj(0), gq_ref[...])
    qs_ref[...] = (q * (A_HEAD_DIM ** -0.5 * math.log2(math.e))).astype(BF16)
    k = head_norm(proj(1), gk_ref[...])
    k32_ref[...] = k
    kb_ref[...] = k.astype(BF16)
    v = proj(2)
    v32_ref[...] = v
    if tk is None:
        vb_ref[...] = v.astype(BF16)
    else:
        vt = _dot_nt(wvt_ref[...], hb).astype(BF16)
        for c in range(vb_ref.shape[0]):
            vb_ref[c] = vt[:, c * tk:(c + 1) * tk]

    lbl = lbl_ref[...]
    e = jnp.exp(lbl - jnp.max(lbl, axis=0, keepdims=True))
    lb = jnp.sum(e[:layer + 1], axis=0, keepdims=True) / jnp.sum(e, axis=0, keepdims=True)

    qb = proj(3)
    hq_ref[...] = (qb * _sigmoid(qb)).astype(BF16)
    f = lb + (1.0 - lb) * _sigmoid(proj(4))
    hk_ref[...] = (1.0 - f).astype(BF16)
    lf_ref[...] = jnp.log(f)
    hv_ref[...] = proj(5).astype(BF16)
    gb = proj(6)
    gt_ref[...] = (gb * _sigmoid(gb)).astype(BF16)


def _mixer_in(x2d, norm_g, w_in_bf, gq, gk, lb_logits, layer, tm, tk):
    t, d = x2d.shape
    w = A_WIDTH
    seg = (jnp.arange(w)[:, None] // A_HEAD_DIM == jnp.arange(w)[None, :] // A_HEAD_DIM).astype(BF16)
    row = lambda i: (i, 0)
    out_dt = [BF16, BF16, BF16, F32, F32, BF16, BF16, BF16, F32, BF16]
    out_shape = [jax.ShapeDtypeStruct((t, w), dt) for dt in out_dt]
    out_specs = [pl.BlockSpec((tm, w), row) for _ in out_dt]
    if tk is not None:
        out_shape[2] = jax.ShapeDtypeStruct((t // tk, w, tk), BF16)
        out_specs[2] = pl.BlockSpec((tm // tk, w, tk), lambda i: (i, 0, 0))
    wvt = w_in_bf[:, 2 * w:3 * w].T
    return pl.pallas_call(
        functools.partial(_mixer_in_kernel, layer=layer, tk=tk),
        out_shape=out_shape,
        grid=(t // tm,),
        in_specs=[pl.BlockSpec((tm, d), row), _const_spec((1, d)), _const_spec(w_in_bf.shape),
                  _const_spec(wvt.shape), _const_spec((1, w)), _const_spec((1, w)),
                  _const_spec(lb_logits.shape), _const_spec((w, w))],
        out_specs=out_specs,
        compiler_params=_cparams(("parallel",)),
        name="mixer_in",
    )(x2d, norm_g.reshape(1, d), w_in_bf, wvt,
      jnp.tile(gq, 2 * A_HEADS).reshape(1, w), jnp.tile(gk, 2 * A_HEADS).reshape(1, w),
      lb_logits, seg)


def _lambda(lam_ref, lam0):
    lv = lam_ref[...]
    e1 = jnp.exp(jnp.sum(lv[0:1] * lv[1:2], axis=-1, keepdims=True))
    e2 = jnp.exp(jnp.sum(lv[2:3] * lv[3:4], axis=-1, keepdims=True))
    return e1 - e2 + lam0


def _split_maps(k):
    first = lax.broadcasted_iota(I32, k.shape, 1) < A_HEAD_DIM
    zero = jnp.zeros_like(k)
    return jnp.where(first, k, zero), jnp.where(first, zero, k)


def _attn_prompt_kernel(lam_ref, q_ref, k_ref, vt_ref, o_ref,
                        sa_ref, sb_ref, pa_ref, pb_ref, acc_ref, *, tq, tk, lam0):
    assert tq == 2 * tk
    qi = pl.program_id(2)
    q = q_ref[0]
    s_bufs = (sa_ref, sb_ref)
    p_bufs = (pa_ref, pb_ref)

    def phase(j, par, mx_cur, ml, masked, prefetch):
        s_ref, p_ref = s_bufs[par], p_bufs[par]
        sn_ref, pp_ref = s_bufs[1 - par], p_bufs[1 - par]
        vt = vt_ref[jnp.maximum(j - 1, 0)]
        mx_next = []
        if prefetch:
            k = k_ref[0, pl.ds(pl.multiple_of((j + 1) * tk, tk), tk), :]
            for mp, km in enumerate(_split_maps(k)):
                s = _dot_nt(km, q)
                sn_ref[mp] = s
                mx_next.append(jnp.max(s, axis=0, keepdims=True))
        new_ml = []
        for mp in range(2):
            m, l = ml[mp]
            pv = _dot(vt, pp_ref[mp])
            s = s_ref[mp]
            s_max = mx_cur[mp]
            if masked:
                kc = (j * tk + lax.broadcasted_iota(I32, s.shape, 0)) // CHUNK
                qc = (qi * tq + lax.broadcasted_iota(I32, s.shape, 1)) // CHUNK
                s = jnp.where(kc <= qc, s, -jnp.inf)
                s_max = jnp.max(s, axis=0, keepdims=True)
            m_new = jnp.maximum(m, s_max)
            alpha = jnp.exp2(m - m_new)
            p = jnp.exp2(s - m_new)
            p_ref[mp] = p.astype(BF16)
            acc_ref[mp] = alpha * (acc_ref[mp] + pv)
            new_ml.append((m_new, alpha * l + jnp.sum(p, axis=0, keepdims=True)))
        return tuple(mx_next), tuple(new_ml)

    k0 = k_ref[0, pl.ds(0, tk), :]
    mx = []
    for mp, km in enumerate(_split_maps(k0)):
        s = _dot_nt(km, q)
        sa_ref[mp] = s
        mx.append(jnp.max(s, axis=0, keepdims=True))
    pb_ref[...] = jnp.zeros_like(pb_ref)
    acc_ref[...] = jnp.zeros_like(acc_ref)
    ml = tuple((jnp.full((1, tq), -jnp.inf, F32), jnp.zeros((1, tq), F32)) for _ in range(2))

    def pair(t, carry):
        mx_a, ml = carry
        mx_b, ml = phase(2 * t, 0, mx_a, ml, False, True)
        mx_a, ml = phase(2 * t + 1, 1, mx_b, ml, False, True)
        return mx_a, ml

    mx_a, ml = lax.fori_loop(0, qi, pair, (tuple(mx), ml))
    mx_b, ml = phase(2 * qi, 0, mx_a, ml, True, True)
    _, ml = phase(2 * qi + 1, 1, mx_b, ml, True, False)
    vt = vt_ref[2 * qi + 1]
    (_, l1), (_, l2) = ml
    o1 = (acc_ref[0] + _dot(vt, pb_ref[0])) / l1
    o2 = (acc_ref[1] + _dot(vt, pb_ref[1])) / l2
    o_ref[0] = (o1 - _lambda(lam_ref, lam0) * o2).T.astype(o_ref.dtype)


def _attn_prompt(lam_vec, qs, kb, vt, lam0, tq, tk):
    b, s, w = qs.shape
    nkb = s // tk
    blk = lambda bi, hi, qi: (bi, qi, hi)
    return pl.pallas_call(
        functools.partial(_attn_prompt_kernel, tq=tq, tk=tk, lam0=lam0),
        out_shape=jax.ShapeDtypeStruct((b, s, w), F32),
        grid=(b, A_HEADS, s // tq),
        in_specs=[_const_spec(lam_vec.shape), pl.BlockSpec((1, tq, LANES), blk),
                  pl.BlockSpec((1, s, LANES), lambda bi, hi, qi: (bi, 0, hi)),
                  pl.BlockSpec((nkb, LANES, tk), lambda bi, hi, qi: (bi, hi, 0))],
        out_specs=pl.BlockSpec((1, tq, LANES), blk),
        scratch_shapes=[pltpu.VMEM((2, tk, tq), F32), pltpu.VMEM((2, tk, tq), F32),
                        pltpu.VMEM((2, tk, tq), BF16), pltpu.VMEM((2, tk, tq), BF16),
                        pltpu.VMEM((2, LANES, tq), F32)],
        compiler_params=_cparams(("parallel", "parallel", "arbitrary")),
        name="attn_prompt",
    )(lam_vec, qs, kb, vt)


def _attn_sample_kernel(lam_ref, q_ref, kc_ref, vc_ref, kn_ref, vn_ref, o_ref, *, lam0):
    q = q_ref[0]
    vc = vc_ref[0]
    vn = vn_ref[0]
    outs = []
    for kcm, knm in zip(_split_maps(kc_ref[0]), _split_maps(kn_ref[0])):
        sc = _dot_nt(q, kcm)
        sn = _dot_nt(q, knm)
        m = jnp.maximum(jnp.max(sc, axis=-1, keepdims=True), jnp.max(sn, axis=-1, keepdims=True))
        pc = jnp.exp2(sc - m)
        pn = jnp.exp2(sn - m)
        l = jnp.sum(pc, axis=-1, keepdims=True) + jnp.sum(pn, axis=-1, keepdims=True)
        outs.append((_dot(pc.astype(BF16), vc) + _dot(pn.astype(BF16), vn)) / l)
    o_ref[0] = (outs[0] - _lambda(lam_ref, lam0) * outs[1]).astype(o_ref.dtype)


def _attn_sample(lam_vec, qs, kc, vc, kn, vn, lam0):
    b, t, w = qs.shape
    past = kc.shape[1]
    blk = lambda bi, hi: (bi, 0, hi)
    return pl.pallas_call(
        functools.partial(_attn_sample_kernel, lam0=lam0),
        out_shape=jax.ShapeDtypeStruct((b, t, w), F32),
        grid=(b, A_HEADS),
        in_specs=[_const_spec(lam_vec.shape), pl.BlockSpec((1, t, LANES), blk),
                  pl.BlockSpec((1, past, LANES), blk), pl.BlockSpec((1, past, LANES), blk),
                  pl.BlockSpec((1, t, LANES), blk), pl.BlockSpec((1, t, LANES), blk)],
        out_specs=pl.BlockSpec((1, t, LANES), blk),
        compiler_params=_cparams(("parallel", "parallel")),
        name="attn_sample",
    )(lam_vec, qs, kc, vc, kn, vn)


def _cumsum_rows(x):
    c = x.shape[0]
    row = lax.broadcasted_iota(I32, x.shape, 0)
    d = 1
    while d < c:
        x = x + jnp.where(row >= d, pltpu.roll(x, d, axis=0), 0.0)
        d *= 2
    return x


def _block_ref_rows(cum, m):
    c, n = cum.shape
    if m >= 8:
        r = cum.reshape(c // (2 * m), 2 * m, n)[:, m - 1:m, :]
        return jnp.broadcast_to(r, (c // (2 * m), 2 * m, n)).reshape(c, n)
    pos = lax.broadcasted_iota(I32, cum.shape, 0) % (2 * m)
    out = cum
    for p in range(2 * m):
        shift = p - (m - 1)
        if shift != 0:
            out = jnp.where(pos == p, pltpu.roll(cum, shift % c, axis=0), out)
    return out


def _hgrn_kernel(hq_ref, hk_ref, hv_ref, lf_ref, s0_ref, o_ref, st_ref, stt_ref, *, c):
    ci = pl.program_id(1)

    @pl.when(ci == 0)
    def _():
        for h in range(B_HEADS):
            stt_ref[h] = s0_ref[0, h].T

    row = lax.broadcasted_iota(I32, (c, B_DIM), 0)
    r2 = lax.broadcasted_iota(I32, (c, c), 0)
    c2 = lax.broadcasted_iota(I32, (c, c), 1)
    for h in range(B_HEADS):
        sl = slice(h * B_DIM, (h + 1) * B_DIM)
        hq = hq_ref[0, :, sl].astype(F32)
        hk = hk_ref[0, :, sl].astype(F32)
        hv = hv_ref[0, :, sl]
        cum = _cumsum_rows(lf_ref[0, :, sl])
        last = cum[c - 1:c, :]

        a = jnp.zeros((c, c), F32)
        m = c // 2
        while m >= 1:
            ref = _block_ref_rows(cum, m)
            upper = (row % (2 * m)) >= m
            qd = jnp.where(upper, hq * jnp.exp(jnp.minimum(cum - ref, 0.0)), 0.0)
            kd = jnp.where(upper, 0.0, hk * jnp.exp(jnp.minimum(ref - cum, 0.0)))
            p = _dot_nt(qd.astype(BF16), kd.astype(BF16))
            a = a + jnp.where((r2 // (2 * m)) == (c2 // (2 * m)), p, 0.0)
            m //= 2

        stt = stt_ref[h]
        o = _dot_nt((hq * jnp.exp(cum)).astype(BF16), stt.astype(BF16))
        o = o + _dot(a.astype(BF16), hv)
        o = o + jnp.sum(hq * hk, axis=-1, keepdims=True) * hv.astype(F32)
        o_ref[0, :, sl] = o.astype(o_ref.dtype)

        kl = (hk * jnp.exp(last - cum)).astype(BF16)
        stt_ref[h] = stt * jnp.exp(last) + _dot_tn(hv, kl)

    @pl.when(ci == pl.num_programs(1) - 1)
    def _():
        for h in range(B_HEADS):
            st_ref[0, h] = stt_ref[h].T


def _hgrn(hq, hk, hv, lf, s0, c):
    b, s, w = hq.shape
    blk = lambda bi, ci: (bi, ci, 0)
    sblk = lambda bi, ci: (bi, 0, 0, 0)
    return pl.pallas_call(
        functools.partial(_hgrn_kernel, c=c),
        out_shape=[jax.ShapeDtypeStruct((b, s, w), F32),
                   jax.ShapeDtypeStruct((b, B_HEADS, B_DIM, B_DIM), F32)],
        grid=(b, s // c),
        in_specs=[pl.BlockSpec((1, c, w), blk)] * 4 + [pl.BlockSpec((1, B_HEADS, B_DIM, B_DIM), sblk)],
        out_specs=[pl.BlockSpec((1, c, w), blk), pl.BlockSpec((1, B_HEADS, B_DIM, B_DIM), sblk)],
        scratch_shapes=[pltpu.VMEM((B_HEADS, B_DIM, B_DIM), F32)],
        compiler_params=_cparams(("parallel", "arbitrary")),
        name="hgrn",
    )(hq, hk, hv, lf, s0)


def _mem_kv_kernel(m_ref, g_ref, wk_ref, wv_ref, gk_ref, k_ref, v_ref):
    hm = _rms(m_ref[...], g_ref[...]).astype(BF16)
    k = _dot(hm, wk_ref[...])
    gk = gk_ref[...]
    hd = k.shape[-1] // X_HEADS
    for h in range(X_HEADS):
        sl = slice(h * hd, (h + 1) * hd)
        k_ref[:, sl] = _rms(k[:, sl], gk)
    v_ref[...] = _dot(hm, wv_ref[...])


def _mem_kv(mem2d, norm_g, w_mk_bf, w_mv_bf, xk_g, tm):
    t, d = mem2d.shape
    row = lambda i: (i, 0)
    return pl.pallas_call(
        _mem_kv_kernel,
        out_shape=[jax.ShapeDtypeStruct((t, d), F32)] * 2,
        grid=(t // tm,),
        in_specs=[pl.BlockSpec((tm, d), row), _const_spec((1, d)), _const_spec(w_mk_bf.shape),
                  _const_spec(w_mv_bf.shape), _const_spec((1, xk_g.shape[-1]))],
        out_specs=[pl.BlockSpec((tm, d), row)] * 2,
        compiler_params=_cparams(("parallel",)),
        name="mem_kv",
    )(mem2d, norm_g.reshape(1, d), w_mk_bf, w_mv_bf, xk_g.reshape(1, -1))


def _post_kernel(x_ref, oa_ref, ob_ref, gt_ref, sg_ref, gg_ref, wo_ref,
                 nx_ref, wq_ref, xq_ref, mk_ref, mv_ref, wxo_ref,
                 nf_ref, wr_ref, br_ref,
                 x2_ref, xs_ref, rloc_ref, gate_ref, cnt_ref, *, lam0):
    ya = []
    yb = []
    for h in range(A_HEADS):
        sl = slice(h * LANES, (h + 1) * LANES)
        ya.append((_rms(oa_ref[:, sl], sg_ref[...]) * (1.0 - lam0)).astype(BF16))
        yb.append((_rms(ob_ref[:, sl], gg_ref[...]) * gt_ref[:, sl].astype(F32)).astype(BF16))
    y = jnp.concatenate(ya + yb, axis=-1)
    x1 = x_ref[...] + _dot(y, wo_ref[...])

    q = _dot(_rms(x1, nx_ref[...]).astype(BF16), wq_ref[...])
    hd = q.shape[-1] // X_HEADS
    os_ = []
    for h in range(X_HEADS):
        sl = slice(h * hd, (h + 1) * hd)
        qh = _rms(q[:, sl], xq_ref[...]).astype(BF16)
        s = _dot_nt(qh, mk_ref[0, :, sl]) * (hd ** -0.5)
        p = jnp.exp(s - jnp.max(s, axis=-1, keepdims=True))
        p = p / jnp.sum(p, axis=-1, keepdims=True)
        os_.append(_dot(p.astype(BF16), mv_ref[0, :, sl]).astype(BF16))
    x2 = x1 + _dot(jnp.concatenate(os_, axis=-1), wxo_ref[...])
    x2_ref[...] = x2

    h2 = _rms(x2, nf_ref[...])
    w = h2.shape[0]
    logits = lax.dot_general(wr_ref[...], h2, (((1,), (1,)), ((), ())), preferred_element_type=F32,
                             precision=lax.Precision.HIGHEST) + br_ref[...]
    erow = lax.broadcasted_iota(I32, logits.shape, 0)
    picks, vals = [], []
    for k in range(TOP_K):
        mx = jnp.max(logits, axis=0, keepdims=True)
        ix = jnp.min(jnp.where(logits == mx, erow, N_EXPERTS), axis=0, keepdims=True)
        pick = erow == ix
        picks.append(pick)
        vals.append(mx)
        logits = jnp.where(pick, -jnp.inf, logits)
    ex = [jnp.exp(v - vals[0]) for v in vals]
    den = ex[0] + ex[1] + ex[2] + ex[3]

    onehot = jnp.where(picks[0] | picks[1] | picks[2] | picks[3], 1.0, 0.0)
    earlier = (lax.broadcasted_iota(I32, (w, w), 0) < lax.broadcasted_iota(I32, (w, w), 1))
    rank = _dot(onehot.astype(BF16), jnp.where(earlier, 1.0, 0.0).astype(BF16))
    cnt = jnp.sum(onehot, axis=1, keepdims=True).astype(I32)
    pcnt = ((cnt + (ROW_BLOCK - 1)) // ROW_BLOCK) * ROW_BLOCK
    pc_b = jnp.broadcast_to(pcnt.astype(F32), onehot.shape)
    seg_off = _cumsum_rows(pc_b) - pc_b
    cap = xs_ref.shape[0]
    riota = lax.broadcasted_iota(I32, (cap, w), 0)
    sel = None
    lane8 = lax.broadcasted_iota(I32, rloc_ref.shape, 0)
    rloc_out = jnp.zeros(rloc_ref.shape, I32)
    gate_out = jnp.zeros(gate_ref.shape, F32)
    for k in range(TOP_K):
        rloc = jnp.sum(jnp.where(picks[k], seg_off + rank, 0.0), axis=0, keepdims=True).astype(I32)
        hit = riota == rloc
        sel = hit if sel is None else (sel | hit)
        rloc_out = jnp.where(lane8 == k, rloc, rloc_out)
        gate_out = jnp.where(lane8 == k, ex[k] / den, gate_out)
    xs_ref[...] = _dot(jnp.where(sel, 1.0, 0.0).astype(BF16), h2.astype(BF16))
    rloc_ref[...] = rloc_out
    gate_ref[...] = gate_out
    cnt_ref[...] = jnp.broadcast_to(cnt, cnt_ref.shape)


def _window_cap(w):
    cap = w * TOP_K + N_EXPERTS * (ROW_BLOCK - 1) + 2 * ROW_BLOCK
    return -(-cap // LANES) * LANES


def _post(x2d, oa, ob, gt, subln_g, gn_g, w_out_bf, norm_x_g, w_xq_bf, xq_g, mk_bf, mv_bf,
          w_xo_bf, norm_ffn_g, w_router, b_router, lam0, tm, rows_per_batch):
    t, d = x2d.shape
    n_win = t // tm
    cap = _window_cap(tm)
    row = lambda i: (i, 0)
    tiles_per_batch = rows_per_batch // tm
    mem = lambda i: (i // tiles_per_batch, 0, 0)
    n_mem = mk_bf.shape[1]
    vec = lambda a: a.reshape(1, -1)
    return pl.pallas_call(
        functools.partial(_post_kernel, lam0=lam0),
        out_shape=[jax.ShapeDtypeStruct((t, d), F32),
                   jax.ShapeDtypeStruct((n_win * cap, d), F32),
                   jax.ShapeDtypeStruct((n_win * 8, tm), I32),
                   jax.ShapeDtypeStruct((n_win * 8, tm), F32),
                   jax.ShapeDtypeStruct((n_win * N_EXPERTS, LANES), I32)],
        grid=(n_win,),
        in_specs=[pl.BlockSpec((tm, d), row), pl.BlockSpec((tm, A_WIDTH), row),
                  pl.BlockSpec((tm, A_WIDTH), row), pl.BlockSpec((tm, A_WIDTH), row),
                  _const_spec((1, LANES)), _const_spec((1, LANES)), _const_spec(w_out_bf.shape),
                  _const_spec((1, d)), _const_spec(w_xq_bf.shape), _const_spec((1, xq_g.shape[-1])),
                  pl.BlockSpec((1, n_mem, d), mem), pl.BlockSpec((1, n_mem, d), mem),
                  _const_spec(w_xo_bf.shape),
                  _const_spec((1, d)), _const_spec((N_EXPERTS, d)), _const_spec((N_EXPERTS, 1))],
        out_specs=[pl.BlockSpec((tm, d), row), pl.BlockSpec((cap, d), row),
                   pl.BlockSpec((8, tm), row), pl.BlockSpec((8, tm), row),
                   pl.BlockSpec((N_EXPERTS, LANES), row)],
        compiler_params=_cparams(("parallel",)),
        name="post",
    )(x2d, oa, ob, gt, vec(subln_g), vec(gn_g), w_out_bf, vec(norm_x_g), w_xq_bf, vec(xq_g),
      mk_bf, mv_bf, w_xo_bf, vec(norm_ffn_g), w_router.T, b_router.reshape(N_EXPERTS, 1))


def _moe_expert_kernel(te_ref, tf_ref, tv_ref, g0_ref, gn_ref, sd_ref, xs_hbm,
                       wgu_ref, bgu_ref, wd_ref, bd_ref, ys_hbm,
                       xbuf, ybuf, wgu_bf, wd_bf, gsem, ssem, *, tm):
    i = pl.program_id(0)
    last = pl.num_programs(0) - 1
    slot = i % 2
    nb = tm // ROW_BLOCK
    valid = tv_ref[i] == 1
    next_valid = jnp.logical_and(i < last, tv_ref[jnp.minimum(i + 1, last)] == 1)

    def gather(tab_ref, s):
        for j in range(nb):
            r = pl.multiple_of(tab_ref[0, 0, j], ROW_BLOCK)
            pltpu.make_async_copy(xs_hbm.at[pl.ds(r, ROW_BLOCK)],
                                  xbuf.at[s, pl.ds(j * ROW_BLOCK, ROW_BLOCK)], gsem.at[s]).start()

    def scatter(s):
        for j in range(nb):
            r = pl.multiple_of(sd_ref[0, 0, j], ROW_BLOCK)
            pltpu.make_async_copy(ybuf.at[s, pl.ds(j * ROW_BLOCK, ROW_BLOCK)],
                                  ys_hbm.at[pl.ds(r, ROW_BLOCK)], ssem.at[s]).start()

    def wait_gather(s):
        pltpu.make_async_copy(xs_hbm.at[pl.ds(0, tm)], xbuf.at[s], gsem.at[s]).wait()

    def wait_scatter(s):
        pltpu.make_async_copy(ybuf.at[s], ys_hbm.at[pl.ds(0, tm)], ssem.at[s]).wait()

    @pl.when(jnp.logical_and(i == 0, valid))
    def _():
        gather(g0_ref, 0)

    @pl.when(jnp.logical_and(valid, tf_ref[i] == 1))
    def _():
        wgu_bf[...] = wgu_ref[0].astype(BF16)
        wd_bf[...] = wd_ref[0].astype(BF16)

    @pl.when(valid)
    def _():
        wait_gather(slot)

        @pl.when(next_valid)
        def _():
            gather(gn_ref, 1 - slot)

        de = wd_bf.shape[0]
        gu = _dot(xbuf[slot].astype(BF16), wgu_bf[...]) + bgu_ref[0]
        gate = jnp.minimum(gu[:, :de], SWIGLU_LIMIT)
        up = jnp.clip(gu[:, de:], -SWIGLU_LIMIT, SWIGLU_LIMIT)
        act = (up + 1.0) * gate * _sigmoid(SWIGLU_ALPHA * gate)
        ybuf[slot] = _dot(act.astype(BF16), wd_bf[...]) + bd_ref[0]

        @pl.when(i > 0)
        def _():
            wait_scatter(1 - slot)

        scatter(slot)

        @pl.when(jnp.logical_not(next_valid))
        def _():
            wait_scatter(slot)


def _moe_expert(tables, xs, w_gate_up, b_gate_up, w_down, b_down, tm):
    tile_expert, tile_first, tile_valid, gsrc, sdst = tables
    nt = tile_expert.shape[0]
    nb = tm // ROW_BLOCK
    d = xs.shape[1]
    ne, _, dgu = w_gate_up.shape
    de = w_down.shape[1]
    smem = lambda f: pl.BlockSpec((1, 1, nb), f, memory_space=pltpu.SMEM)
    wmap = lambda i, te, tf, tv: (te[i], 0, 0)
    grid_spec = pltpu.PrefetchScalarGridSpec(
        num_scalar_prefetch=3,
        grid=(nt,),
        in_specs=[smem(lambda i, te, tf, tv: (0, 0, 0)),
                  smem(lambda i, te, tf, tv: (jnp.minimum(i + 1, nt - 1), 0, 0)),
                  smem(lambda i, te, tf, tv: (i, 0, 0)),
                  pl.BlockSpec(memory_space=pl.ANY),
                  pl.BlockSpec((1, d, dgu), wmap), pl.BlockSpec((1, 1, dgu), wmap),
                  pl.BlockSpec((1, de, d), wmap), pl.BlockSpec((1, 1, d), wmap)],
        out_specs=pl.BlockSpec(memory_space=pl.ANY),
        scratch_shapes=[pltpu.VMEM((2, tm, d), F32), pltpu.VMEM((2, tm, d), F32),
                        pltpu.VMEM((d, dgu), BF16), pltpu.VMEM((de, d), BF16),
                        pltpu.SemaphoreType.DMA((2,)), pltpu.SemaphoreType.DMA((2,))],
    )
    return pl.pallas_call(
        functools.partial(_moe_expert_kernel, tm=tm),
        out_shape=jax.ShapeDtypeStruct(xs.shape, F32),
        grid_spec=grid_spec,
        input_output_aliases={6: 0},
        compiler_params=_cparams(("arbitrary",)),
        name="moe_expert",
    )(tile_expert, tile_first, tile_valid, gsrc.reshape(nt, 1, nb), gsrc.reshape(nt, 1, nb),
      sdst.reshape(nt, 1, nb), xs, w_gate_up, b_gate_up.reshape(ne, 1, dgu), w_down,
      b_down.reshape(ne, 1, d))


def _moe_combine_kernel(x_ref, ys_ref, rloc_ref, gate_ref, o_ref):
    cap = ys_ref.shape[0]
    w = x_ref.shape[0]
    riota = lax.broadcasted_iota(I32, (cap, w), 0)
    gt = jnp.zeros((cap, w), F32)
    for k in range(TOP_K):
        gt = jnp.where(riota == rloc_ref[k:k + 1, :], gate_ref[k:k + 1, :], gt)
    o_ref[...] = x_ref[...] + _dot_tn(gt.astype(BF16), ys_ref[...].astype(BF16))


def _moe_combine(x2, ys, rloc, gate, tm):
    t, d = x2.shape
    cap = _window_cap(tm)
    row = lambda i: (i, 0)
    return pl.pallas_call(
        _moe_combine_kernel,
        out_shape=jax.ShapeDtypeStruct((t, d), F32),
        grid=(t // tm,),
        in_specs=[pl.BlockSpec((tm, d), row), pl.BlockSpec((cap, d), row),
                  pl.BlockSpec((8, tm), row), pl.BlockSpec((8, tm), row)],
        out_specs=pl.BlockSpec((tm, d), row),
        compiler_params=_cparams(("parallel",)),
        name="moe_combine",
    )(x2, ys, rloc, gate)


def _moe_tables(cnt, cap, n_picks, tm_e):
    n_win = cnt.shape[0]
    bpt = tm_e // ROW_BLOCK
    nt = (n_picks // ROW_BLOCK + n_win * N_EXPERTS) // bpt + N_EXPERTS
    blk = (cnt + ROW_BLOCK - 1) // ROW_BLOCK
    seg_blk = jnp.cumsum(blk, axis=1) - blk
    cum_incl = jnp.cumsum(blk, axis=0)
    cum_excl = cum_incl - blk
    tot = cum_incl[-1]
    tiles = (tot + bpt - 1) // bpt
    tile_end = jnp.cumsum(tiles)
    tile_beg = tile_end - tiles
    ti = jnp.arange(nt, dtype=I32)
    te = jnp.minimum(jnp.sum((ti[:, None] >= tile_end[None, :]).astype(I32), axis=1), N_EXPERTS - 1)
    tv = ti < tile_end[-1]
    tf = jnp.concatenate([jnp.ones((1,), bool), te[1:] != te[:-1]])
    v = (ti - tile_beg[te])[:, None] * bpt + jnp.arange(bpt, dtype=I32)[None, :]
    ok = tv[:, None] & (v < tot[te][:, None])
    widx = jnp.sum((cum_incl.T[te][:, None, :] <= v[:, :, None]).astype(I32), axis=-1)
    onehot = jnp.minimum(widx, n_win - 1)[:, :, None] == jnp.arange(n_win, dtype=I32)
    base = (jnp.arange(n_win, dtype=I32) * cap)[None, :] + (seg_blk.T[te] - cum_excl.T[te]) * ROW_BLOCK
    row = jnp.sum(jnp.where(onehot, base[:, None, :], 0), axis=-1) + v * ROW_BLOCK
    assert n_win >= bpt
    spare = jnp.arange(bpt, dtype=I32)[None, :] * cap + (cap - 2 * ROW_BLOCK)
    gsrc = jnp.where(ok, row, cap - ROW_BLOCK).astype(I32)
    sdst = jnp.where(ok, row, spare).astype(I32)
    return te.astype(I32), tf.astype(I32), tv.astype(I32), gsrc, sdst


def _moe(x2, xs, rloc, gate, cnt, w_gate_up, b_gate_up, w_down, b_down, tm_w, tm_e):
    n_win = x2.shape[0] // tm_w
    cnt2 = cnt.reshape(n_win, N_EXPERTS, LANES)[:, :, 0]
    tables = _moe_tables(cnt2, _window_cap(tm_w), x2.shape[0] * TOP_K, tm_e)
    ys = _moe_expert(tables, xs, w_gate_up, b_gate_up, w_down, b_down, tm_e)
    return _moe_combine(x2, ys, rloc, gate, tm_w)


def _pick(n, pref):
    return pref if n % pref == 0 else n


def _group(x, mk_bf, mv_bf, attn_fn, attn_tk, s0, l, p, bf, lam0):
    b, t, d = x.shape
    x2d = x.reshape(b * t, d)
    (qs, kb, vb, k32, v32, hq, hk, hv, lf, gt) = _mixer_in(
        x2d, p["norm_mix_g"][l], bf["w_in"][l], p["a_qnorm_g"][l], p["a_knorm_g"][l],
        p["hgrn_lb_logits"], l, _pick(b * t, 512), attn_tk)
    r3 = lambda a: a.reshape(b, t, -1)
    oa = attn_fn(r3(qs), r3(kb), vb if attn_tk else r3(vb))
    ob, s_new = _hgrn(r3(hq), r3(hk), r3(hv), r3(lf), s0, _pick(t, 64))
    tm_w = _pick(t, 256)
    tm_e = 256 if b * t >= 4096 else 64
    x2, xs, rloc, gate, cnt = _post(
        x2d, oa.reshape(b * t, -1), ob.reshape(b * t, -1), gt, p["a_subln_g"][l], p["b_gnorm_g"][l],
        bf["w_out"][l], p["norm_x_g"][l], bf["w_xq"][l], p["xq_norm_g"][l], mk_bf, mv_bf,
        bf["w_xo"][l], p["norm_ffn_g"][l], p["w_router"][l], p["b_router"][l], lam0,
        tm_w, t)
    y = _moe(x2, xs, rloc, gate, cnt, p["w_gate_up"][l], p["b_gate_up"][l], p["w_down"][l],
             p["b_down"][l], tm_w, tm_e)
    return (y.reshape(b, t, d), k32.reshape(b, t, 2 * A_HEADS, A_HEAD_DIM),
            v32.reshape(b, t, A_HEADS, 2 * A_HEAD_DIM), s_new)


def kernel(x_prompt, x_sample, mem_prompt, cache_attn_k, cache_attn_v, state_hgrn, cache_mem_k, cache_mem_v, norm_mix_g, w_in, a_qnorm_g, a_knorm_g, lambda_q1, lambda_k1, lambda_q2, lambda_k2, a_subln_g, hgrn_lb_logits, b_gnorm_g, w_out, norm_x_g, norm_mem_g, w_xq, w_mk, w_mv, xq_norm_g, xk_norm_g, w_xo, norm_ffn_g, w_router, b_router, w_gate_up, b_gate_up, w_down, b_down):
    p = dict(norm_mix_g=norm_mix_g, a_qnorm_g=a_qnorm_g, a_knorm_g=a_knorm_g, a_subln_g=a_subln_g,
             hgrn_lb_logits=hgrn_lb_logits, b_gnorm_g=b_gnorm_g, norm_x_g=norm_x_g,
             xq_norm_g=xq_norm_g, norm_ffn_g=norm_ffn_g, w_router=w_router, b_router=b_router,
             w_gate_up=w_gate_up, b_gate_up=b_gate_up, w_down=w_down, b_down=b_down)
    bf = dict(w_in=w_in.astype(BF16), w_out=w_out.astype(BF16), w_xq=w_xq.astype(BF16),
              w_xo=w_xo.astype(BF16))
    depth = w_in.shape[0]
    b, s, d = x_prompt.shape
    db, ds, _ = x_sample.shape
    n_mem = mem_prompt.shape[1]
    xp, xs = x_prompt, x_sample
    outs = [[] for _ in range(8)]
    for l in range(depth):
        lam0 = 0.8 - 0.6 * math.exp(-0.3 * l)
        lam_vec = jnp.stack([lambda_q1[l], lambda_k1[l], lambda_q2[l], lambda_k2[l]])

        mk, mv = _mem_kv(mem_prompt.reshape(b * n_mem, d), norm_mem_g[l], w_mk[l].astype(BF16),
                         w_mv[l].astype(BF16), xk_norm_g[l], _pick(b * n_mem, 512))
        tk = 2 * LANES
        attn_p = lambda q, k, vt: _attn_prompt(lam_vec, q, k, vt, lam0, 2 * tk, tk)
        s0 = jnp.zeros((b, B_HEADS, B_DIM, B_DIM), F32)
        xp, kp, vp, sp = _group(xp, mk.reshape(b, n_mem, d).astype(BF16),
                                mv.reshape(b, n_mem, d).astype(BF16), attn_p, tk, s0, l, p, bf, lam0)

        kc = cache_attn_k[l].reshape(db, -1, A_WIDTH).astype(BF16)
        vc = cache_attn_v[l].reshape(db, -1, A_WIDTH).astype(BF16)
        attn_s = lambda q, k, v: _attn_sample(lam_vec, q, kc, vc, k, v, lam0)
        xs, ks, vs, ss = _group(xs, cache_mem_k[l].reshape(db, n_mem, d).astype(BF16),
                                cache_mem_v[l].reshape(db, n_mem, d).astype(BF16), attn_s, None,
                                state_hgrn[l], l, p, bf, lam0)
        for lst, val in zip(outs, (kp, vp, sp, mk.reshape(b, n_mem, X_HEADS, -1),
                                   mv.reshape(b, n_mem, X_HEADS, -1), ks, vs, ss)):
            lst.append(val)
    return (xp, xs) + tuple(jnp.stack(o) for o in outs)
```

```python
import functools
import math

import jax
import jax.numpy as jnp
from jax import lax
from jax.experimental import pallas as pl
from jax.experimental.pallas import tpu as pltpu

F32 = jnp.float32
BF16 = jnp.bfloat16
I32 = jnp.int32

EPS = 1e-6
CHUNK = 64
A_HEADS = 4
A_HEAD_DIM = 64
A_WIDTH = A_HEADS * 2 * A_HEAD_DIM
B_HEADS = 4
B_DIM = 128
X_HEADS = 4
N_EXPERTS = 32
TOP_K = 4
SWIGLU_LIMIT = 7.0
SWIGLU_ALPHA = 1.702

LANES = 128
ROW_BLOCK = 8
VMEM_LIMIT = 56 * 1024 * 1024


def _cparams(sem):
    return pltpu.CompilerParams(dimension_semantics=sem, vmem_limit_bytes=VMEM_LIMIT)


def _const_spec(shape):
    zeros = (0,) * len(shape)
    return pl.BlockSpec(shape, lambda *_: zeros)


def _rms(x, g):
    return x * lax.rsqrt(jnp.mean(x * x, axis=-1, keepdims=True) + EPS) * g


def _dot(a, b):
    return jnp.dot(a, b, preferred_element_type=F32)


def _dot_nt(a, b):
    return lax.dot_general(a, b, (((1,), (1,)), ((), ())), preferred_element_type=F32)


def _dot_tn(a, b):
    return lax.dot_general(a, b, (((0,), (0,)), ((), ())), preferred_element_type=F32)


def _sigmoid(x):
    return 1.0 / (1.0 + jnp.exp(-x))


def _mixer_in_kernel(x_ref, g_ref, w_ref, wvt_ref, gq_ref, gk_ref, lbl_ref, seg_ref,
                     qs_ref, kb_ref, vb_ref, k32_ref, v32_ref,
                     hq_ref, hk_ref, hv_ref, lf_ref, gt_ref, *, layer, tk):
    hb = _rms(x_ref[...], g_ref[...]).astype(BF16)

    def proj(j):
        return _dot(hb, w_ref[:, j * A_WIDTH:(j + 1) * A_WIDTH])

    seg = seg_ref[...]

    def head_norm(a, g):
        sq = a * a
        hi = sq.astype(BF16)
        lo = (sq - hi.astype(F32)).astype(BF16)
        ss = _dot(hi, seg) + _dot(lo, seg)
        return a * lax.rsqrt(ss * (1.0 / A_HEAD_DIM) + EPS) * g

    q = head_norm(proj(0), gq_ref[...])
    qs_ref[...] = (q * (A_HEAD_DIM ** -0.5 * math.log2(math.e))).astype(BF16)
    k = head_norm(proj(1), gk_ref[...])
    for hh in range(2 * A_HEADS):
        k32_ref[:, hh, :] = k[:, hh * A_HEAD_DIM:(hh + 1) * A_HEAD_DIM]
    kb_ref[...] = k.astype(BF16)
    v = proj(2)
    for hh in range(A_HEADS):
        v32_ref[:, hh, :] = v[:, hh * LANES:(hh + 1) * LANES]
    if tk is None:
        vb_ref[...] = v.astype(BF16)
    else:
        vt = _dot_nt(wvt_ref[...], hb).astype(BF16)
        for c in range(vb_ref.shape[0]):
            vb_ref[c] = vt[:, c * tk:(c + 1) * tk]

    lbl = lbl_ref[...]
    e = jnp.exp(lbl - jnp.max(lbl, axis=0, keepdims=True))
    lb = jnp.sum(e[:layer + 1], axis=0, keepdims=True) / jnp.sum(e, axis=0, keepdims=True)

    qb = proj(3)
    hq_ref[...] = (qb * _sigmoid(qb)).astype(BF16)
    f = lb + (1.0 - lb) * _sigmoid(proj(4))
    hk_ref[...] = (1.0 - f).astype(BF16)
    lf_ref[...] = jnp.log(f)
    hv_ref[...] = proj(5).astype(BF16)
    gb = proj(6)
    gt_ref[...] = (gb * _sigmoid(gb)).astype(BF16)


def _mixer_in(x2d, norm_g, w_in_bf, gq, gk, lb_logits, layer, tm, tk):
    t, d = x2d.shape
    w = A_WIDTH
    seg = (jnp.arange(w)[:, None] // A_HEAD_DIM == jnp.arange(w)[None, :] // A_HEAD_DIM).astype(BF16)
    row = lambda i: (i, 0)
    out_dt = [BF16, BF16, BF16, F32, F32, BF16, BF16, BF16, F32, BF16]
    out_shape = [jax.ShapeDtypeStruct((t, w), dt) for dt in out_dt]
    out_specs = [pl.BlockSpec((tm, w), row) for _ in out_dt]
    row3 = lambda i: (i, 0, 0)
    out_shape[3] = jax.ShapeDtypeStruct((t, 2 * A_HEADS, A_HEAD_DIM), F32)
    out_specs[3] = pl.BlockSpec((tm, 2 * A_HEADS, A_HEAD_DIM), row3)
    out_shape[4] = jax.ShapeDtypeStruct((t, A_HEADS, 2 * A_HEAD_DIM), F32)
    out_specs[4] = pl.BlockSpec((tm, A_HEADS, 2 * A_HEAD_DIM), row3)
    if tk is not None:
        out_shape[2] = jax.ShapeDtypeStruct((t // tk, w, tk), BF16)
        out_specs[2] = pl.BlockSpec((tm // tk, w, tk), lambda i: (i, 0, 0))
    wvt = w_in_bf[:, 2 * w:3 * w].T
    return pl.pallas_call(
        functools.partial(_mixer_in_kernel, layer=layer, tk=tk),
        out_shape=out_shape,
        grid=(t // tm,),
        in_specs=[pl.BlockSpec((tm, d), row), _const_spec((1, d)), _const_spec(w_in_bf.shape),
                  _const_spec(wvt.shape), _const_spec((1, w)), _const_spec((1, w)),
                  _const_spec(lb_logits.shape), _const_spec((w, w))],
        out_specs=out_specs,
        compiler_params=_cparams(("parallel",)),
        name="mixer_in",
    )(x2d, norm_g.reshape(1, d), w_in_bf, wvt,
      jnp.tile(gq, 2 * A_HEADS).reshape(1, w), jnp.tile(gk, 2 * A_HEADS).reshape(1, w),
      lb_logits, seg)


def _lambda(lam_ref, lam0):
    lv = lam_ref[...]
    e1 = jnp.exp(jnp.sum(lv[0:1] * lv[1:2], axis=-1, keepdims=True))
    e2 = jnp.exp(jnp.sum(lv[2:3] * lv[3:4], axis=-1, keepdims=True))
    return e1 - e2 + lam0


def _split_maps(k):
    first = lax.broadcasted_iota(I32, k.shape, 1) < A_HEAD_DIM
    zero = jnp.zeros_like(k)
    return jnp.where(first, k, zero), jnp.where(first, zero, k)


def _attn_prompt_kernel(lam_ref, q_ref, k_ref, vt_ref, o_ref,
                        sa_ref, sb_ref, pa_ref, pb_ref, acc_ref, *, tq, tk, lam0):
    assert tq == 2 * tk
    qi = pl.program_id(2)
    q = q_ref[0]
    s_bufs = (sa_ref, sb_ref)
    p_bufs = (pa_ref, pb_ref)

    def phase(j, par, mx_cur, ml, masked, prefetch):
        s_ref, p_ref = s_bufs[par], p_bufs[par]
        sn_ref, pp_ref = s_bufs[1 - par], p_bufs[1 - par]
        vt = vt_ref[jnp.maximum(j - 1, 0)]
        mx_next = []
        if prefetch:
            k = k_ref[0, pl.ds(pl.multiple_of((j + 1) * tk, tk), tk), :]
            for mp, km in enumerate(_split_maps(k)):
                s = _dot_nt(km, q)
                sn_ref[mp] = s
                mx_next.append(jnp.max(s, axis=0, keepdims=True))
        new_ml = []
        for mp in range(2):
            m, l = ml[mp]
            pv = _dot(vt, pp_ref[mp])
            s = s_ref[mp]
            s_max = mx_cur[mp]
            if masked:
                kc = (j * tk + lax.broadcasted_iota(I32, s.shape, 0)) // CHUNK
                qc = (qi * tq + lax.broadcasted_iota(I32, s.shape, 1)) // CHUNK
                s = jnp.where(kc <= qc, s, -jnp.inf)
                s_max = jnp.max(s, axis=0, keepdims=True)
            m_new = jnp.maximum(m, s_max)
            alpha = jnp.exp2(m - m_new)
            p = jnp.exp2(s - m_new)
            p_ref[mp] = p.astype(BF16)
            acc_ref[mp] = alpha * (acc_ref[mp] + pv)
            new_ml.append((m_new, alpha * l + jnp.sum(p, axis=0, keepdims=True)))
        return tuple(mx_next), tuple(new_ml)

    k0 = k_ref[0, pl.ds(0, tk), :]
    mx = []
    for mp, km in enumerate(_split_maps(k0)):
        s = _dot_nt(km, q)
        sa_ref[mp] = s
        mx.append(jnp.max(s, axis=0, keepdims=True))
    pb_ref[...] = jnp.zeros_like(pb_ref)
    acc_ref[...] = jnp.zeros_like(acc_ref)
    ml = tuple((jnp.full((1, tq), -jnp.inf, F32), jnp.zeros((1, tq), F32)) for _ in range(2))

    def pair(t, carry):
        mx_a, ml = carry
        mx_b, ml = phase(2 * t, 0, mx_a, ml, False, True)
        mx_a, ml = phase(2 * t + 1, 1, mx_b, ml, False, True)
        return mx_a, ml

    mx_a, ml = lax.fori_loop(0, qi, pair, (tuple(mx), ml))
    mx_b, ml = phase(2 * qi, 0, mx_a, ml, True, True)
    _, ml = phase(2 * qi + 1, 1, mx_b, ml, True, False)
    vt = vt_ref[2 * qi + 1]
    (_, l1), (_, l2) = ml
    o1 = (acc_ref[0] + _dot(vt, pb_ref[0])) / l1
    o2 = (acc_ref[1] + _dot(vt, pb_ref[1])) / l2
    o_ref[0] = (o1 - _lambda(lam_ref, lam0) * o2).T.astype(o_ref.dtype)


def _attn_prompt(lam_vec, qs, kb, vt, lam0, tq, tk):
    b, s, w = qs.shape
    nkb = s // tk
    blk = lambda bi, hi, qi: (bi, qi, hi)
    return pl.pallas_call(
        functools.partial(_attn_prompt_kernel, tq=tq, tk=tk, lam0=lam0),
        out_shape=jax.ShapeDtypeStruct((b, s, w), F32),
        grid=(b, A_HEADS, s // tq),
        in_specs=[_const_spec(lam_vec.shape), pl.BlockSpec((1, tq, LANES), blk),
                  pl.BlockSpec((1, s, LANES), lambda bi, hi, qi: (bi, 0, hi)),
                  pl.BlockSpec((nkb, LANES, tk), lambda bi, hi, qi: (bi, hi, 0))],
        out_specs=pl.BlockSpec((1, tq, LANES), blk),
        scratch_shapes=[pltpu.VMEM((2, tk, tq), F32), pltpu.VMEM((2, tk, tq), F32),
                        pltpu.VMEM((2, tk, tq), BF16), pltpu.VMEM((2, tk, tq), BF16),
                        pltpu.VMEM((2, LANES, tq), F32)],
        compiler_params=_cparams(("parallel", "parallel", "arbitrary")),
        name="attn_prompt",
    )(lam_vec, qs, kb, vt)


def _attn_sample_kernel(lam_ref, q_ref, kc_ref, vc_ref, kn_ref, vn_ref, o_ref, *, lam0):
    q = q_ref[0]
    vc = vc_ref[0]
    vn = vn_ref[0]
    outs = []
    for kcm, knm in zip(_split_maps(kc_ref[0]), _split_maps(kn_ref[0])):
        sc = _dot_nt(q, kcm)
        sn = _dot_nt(q, knm)
        m = jnp.maximum(jnp.max(sc, axis=-1, keepdims=True), jnp.max(sn, axis=-1, keepdims=True))
        pc = jnp.exp2(sc - m)
        pn = jnp.exp2(sn - m)
        l = jnp.sum(pc, axis=-1, keepdims=True) + jnp.sum(pn, axis=-1, keepdims=True)
        outs.append((_dot(pc.astype(BF16), vc) + _dot(pn.astype(BF16), vn)) / l)
    o_ref[0] = (outs[0] - _lambda(lam_ref, lam0) * outs[1]).astype(o_ref.dtype)


def _attn_sample(lam_vec, qs, kc, vc, kn, vn, lam0):
    b, t, w = qs.shape
    past = kc.shape[1]
    blk = lambda bi, hi: (bi, 0, hi)
    return pl.pallas_call(
        functools.partial(_attn_sample_kernel, lam0=lam0),
        out_shape=jax.ShapeDtypeStruct((b, t, w), F32),
        grid=(b, A_HEADS),
        in_specs=[_const_spec(lam_vec.shape), pl.BlockSpec((1, t, LANES), blk),
                  pl.BlockSpec((1, past, LANES), blk), pl.BlockSpec((1, past, LANES), blk),
                  pl.BlockSpec((1, t, LANES), blk), pl.BlockSpec((1, t, LANES), blk)],
        out_specs=pl.BlockSpec((1, t, LANES), blk),
        compiler_params=_cparams(("parallel", "parallel")),
        name="attn_sample",
    )(lam_vec, qs, kc, vc, kn, vn)


def _cumsum_rows(x):
    c = x.shape[0]
    row = lax.broadcasted_iota(I32, x.shape, 0)
    d = 1
    while d < c:
        x = x + jnp.where(row >= d, pltpu.roll(x, d, axis=0), 0.0)
        d *= 2
    return x


def _block_ref_rows(cum, m):
    c, n = cum.shape
    if m >= 8:
        r = cum.reshape(c // (2 * m), 2 * m, n)[:, m - 1:m, :]
        return jnp.broadcast_to(r, (c // (2 * m), 2 * m, n)).reshape(c, n)
    pos = lax.broadcasted_iota(I32, cum.shape, 0) % (2 * m)
    out = cum
    for p in range(2 * m):
        shift = p - (m - 1)
        if shift != 0:
            out = jnp.where(pos == p, pltpu.roll(cum, shift % c, axis=0), out)
    return out


def _hgrn_kernel(hq_ref, hk_ref, hv_ref, lf_ref, s0_ref, o_ref, st_ref, stt_ref, *, c):
    ci = pl.program_id(1)

    @pl.when(ci == 0)
    def _():
        for h in range(B_HEADS):
            stt_ref[h] = s0_ref[0, h].T

    row = lax.broadcasted_iota(I32, (c, B_DIM), 0)
    r2 = lax.broadcasted_iota(I32, (c, c), 0)
    c2 = lax.broadcasted_iota(I32, (c, c), 1)
    for h in range(B_HEADS):
        sl = slice(h * B_DIM, (h + 1) * B_DIM)
        hq = hq_ref[0, :, sl].astype(F32)
        hk = hk_ref[0, :, sl].astype(F32)
        hv = hv_ref[0, :, sl]
        cum = _cumsum_rows(lf_ref[0, :, sl])
        last = cum[c - 1:c, :]

        a = jnp.zeros((c, c), F32)
        m = c // 2
        while m >= 1:
            ref = _block_ref_rows(cum, m)
            upper = (row % (2 * m)) >= m
            qd = jnp.where(upper, hq * jnp.exp(jnp.minimum(cum - ref, 0.0)), 0.0)
            kd = jnp.where(upper, 0.0, hk * jnp.exp(jnp.minimum(ref - cum, 0.0)))
            p = _dot_nt(qd.astype(BF16), kd.astype(BF16))
            a = a + jnp.where((r2 // (2 * m)) == (c2 // (2 * m)), p, 0.0)
            m //= 2

        stt = stt_ref[h]
        o = _dot_nt((hq * jnp.exp(cum)).astype(BF16), stt.astype(BF16))
        o = o + _dot(a.astype(BF16), hv)
        o = o + jnp.sum(hq * hk, axis=-1, keepdims=True) * hv.astype(F32)
        o_ref[0, :, sl] = o.astype(o_ref.dtype)

        kl = (hk * jnp.exp(last - cum)).astype(BF16)
        stt_ref[h] = stt * jnp.exp(last) + _dot_tn(hv, kl)

    @pl.when(ci == pl.num_programs(1) - 1)
    def _():
        for h in range(B_HEADS):
            st_ref[0, h] = stt_ref[h].T


def _hgrn(hq, hk, hv, lf, s0, c):
    b, s, w = hq.shape
    blk = lambda bi, ci: (bi, ci, 0)
    sblk = lambda bi, ci: (bi, 0, 0, 0)
    return pl.pallas_call(
        functools.partial(_hgrn_kernel, c=c),
        out_shape=[jax.ShapeDtypeStruct((b, s, w), F32),
                   jax.ShapeDtypeStruct((b, B_HEADS, B_DIM, B_DIM), F32)],
        grid=(b, s // c),
        in_specs=[pl.BlockSpec((1, c, w), blk)] * 4 + [pl.BlockSpec((1, B_HEADS, B_DIM, B_DIM), sblk)],
        out_specs=[pl.BlockSpec((1, c, w), blk), pl.BlockSpec((1, B_HEADS, B_DIM, B_DIM), sblk)],
        scratch_shapes=[pltpu.VMEM((B_HEADS, B_DIM, B_DIM), F32)],
        compiler_params=_cparams(("parallel", "arbitrary")),
        name="hgrn",
    )(hq, hk, hv, lf, s0)


def _mem_kv_kernel(m_ref, g_ref, wk_ref, wv_ref, gk_ref, k_ref, v_ref):
    hm = _rms(m_ref[...], g_ref[...]).astype(BF16)
    k = _dot(hm, wk_ref[...])
    gk = gk_ref[...]
    hd = k.shape[-1] // X_HEADS
    for h in range(X_HEADS):
        sl = slice(h * hd, (h + 1) * hd)
        k_ref[:, sl] = _rms(k[:, sl], gk)
    v_ref[...] = _dot(hm, wv_ref[...])


def _mem_kv(mem2d, norm_g, w_mk_bf, w_mv_bf, xk_g, tm):
    t, d = mem2d.shape
    row = lambda i: (i, 0)
    return pl.pallas_call(
        _mem_kv_kernel,
        out_shape=[jax.ShapeDtypeStruct((t, d), F32)] * 2,
        grid=(t // tm,),
        in_specs=[pl.BlockSpec((tm, d), row), _const_spec((1, d)), _const_spec(w_mk_bf.shape),
                  _const_spec(w_mv_bf.shape), _const_spec((1, xk_g.shape[-1]))],
        out_specs=[pl.BlockSpec((tm, d), row)] * 2,
        compiler_params=_cparams(("parallel",)),
        name="mem_kv",
    )(mem2d, norm_g.reshape(1, d), w_mk_bf, w_mv_bf, xk_g.reshape(1, -1))


def _post_kernel(x_ref, oa_ref, ob_ref, gt_ref, sg_ref, gg_ref, wo_ref,
                 nx_ref, wq_ref, xq_ref, mk_ref, mv_ref, wxo_ref,
                 nf_ref, wr_ref, br_ref,
                 x2_ref, xs_ref, rloc_ref, gate_ref, cnt_ref, *, lam0):
    ya = []
    yb = []
    for h in range(A_HEADS):
        sl = slice(h * LANES, (h + 1) * LANES)
        ya.append((_rms(oa_ref[:, sl], sg_ref[...]) * (1.0 - lam0)).astype(BF16))
        yb.append((_rms(ob_ref[:, sl], gg_ref[...]) * gt_ref[:, sl].astype(F32)).astype(BF16))
    y = jnp.concatenate(ya + yb, axis=-1)
    x1 = x_ref[...] + _dot(y, wo_ref[...])

    q = _dot(_rms(x1, nx_ref[...]).astype(BF16), wq_ref[...])
    hd = q.shape[-1] // X_HEADS
    os_ = []
    for h in range(X_HEADS):
        sl = slice(h * hd, (h + 1) * hd)
        qh = _rms(q[:, sl], xq_ref[...]).astype(BF16)
        s = _dot_nt(qh, mk_ref[0, :, sl]) * (hd ** -0.5)
        p = jnp.exp(s - jnp.max(s, axis=-1, keepdims=True))
        p = p / jnp.sum(p, axis=-1, keepdims=True)
        os_.append(_dot(p.astype(BF16), mv_ref[0, :, sl]).astype(BF16))
    x2 = x1 + _dot(jnp.concatenate(os_, axis=-1), wxo_ref[...])
    x2_ref[...] = x2

    h2 = _rms(x2, nf_ref[...])
    w = h2.shape[0]
    h_hi = h2.astype(BF16)
    h_lo = (h2 - h_hi.astype(F32)).astype(BF16)
    wr = wr_ref[...]
    w_hi = wr.astype(BF16)
    w_lo = (wr - w_hi.astype(F32)).astype(BF16)
    a = _dot_nt(jnp.concatenate([w_hi, w_lo], axis=0), h_hi)
    logits = a[:N_EXPERTS] + a[N_EXPERTS:] + _dot_nt(w_hi, h_lo) + br_ref[...]
    erow = lax.broadcasted_iota(I32, logits.shape, 0)
    picks, vals = [], []
    for k in range(TOP_K):
        mx = jnp.max(logits, axis=0, keepdims=True)
        ix = jnp.min(jnp.where(logits == mx, erow, N_EXPERTS), axis=0, keepdims=True)
        pick = erow == ix
        picks.append(pick)
        vals.append(mx)
        logits = jnp.where(pick, -jnp.inf, logits)
    ex = [jnp.exp(v - vals[0]) for v in vals]
    den = ex[0] + ex[1] + ex[2] + ex[3]

    onehot = jnp.where(picks[0] | picks[1] | picks[2] | picks[3], 1.0, 0.0)
    earlier = (lax.broadcasted_iota(I32, (w, w), 0) < lax.broadcasted_iota(I32, (w, w), 1))
    rank = _dot(onehot.astype(BF16), jnp.where(earlier, 1.0, 0.0).astype(BF16))
    cnt = jnp.sum(onehot, axis=1, keepdims=True).astype(I32)
    pcnt = ((cnt + (ROW_BLOCK - 1)) // ROW_BLOCK) * ROW_BLOCK
    pc_b = jnp.broadcast_to(pcnt.astype(F32), onehot.shape)
    seg_off = _cumsum_rows(pc_b) - pc_b
    cap = xs_ref.shape[0]
    riota = lax.broadcasted_iota(I32, (cap, w), 0)
    sel = None
    lane8 = lax.broadcasted_iota(I32, rloc_ref.shape, 0)
    rloc_out = jnp.zeros(rloc_ref.shape, I32)
    gate_out = jnp.zeros(gate_ref.shape, F32)
    for k in range(TOP_K):
        rloc = jnp.sum(jnp.where(picks[k], seg_off + rank, 0.0), axis=0, keepdims=True).astype(I32)
        hit = riota == rloc
        sel = hit if sel is None else (sel | hit)
        rloc_out = jnp.where(lane8 == k, rloc, rloc_out)
        gate_out = jnp.where(lane8 == k, ex[k] / den, gate_out)
    xs_ref[...] = _dot(jnp.where(sel, 1.0, 0.0).astype(BF16), h_hi)
    rloc_ref[...] = rloc_out
    gate_ref[...] = gate_out
    cnt_ref[...] = jnp.broadcast_to(cnt, cnt_ref.shape)


def _window_cap(w):
    cap = w * TOP_K + N_EXPERTS * (ROW_BLOCK - 1) + 2 * ROW_BLOCK
    return -(-cap // LANES) * LANES


def _post(x2d, oa, ob, gt, subln_g, gn_g, w_out_bf, norm_x_g, w_xq_bf, xq_g, mk_bf, mv_bf,
          w_xo_bf, norm_ffn_g, w_router, b_router, lam0, tm, rows_per_batch):
    t, d = x2d.shape
    n_win = t // tm
    cap = _window_cap(tm)
    row = lambda i: (i, 0)
    tiles_per_batch = rows_per_batch // tm
    mem = lambda i: (i // tiles_per_batch, 0, 0)
    n_mem = mk_bf.shape[1]
    vec = lambda a: a.reshape(1, -1)
    return pl.pallas_call(
        functools.partial(_post_kernel, lam0=lam0),
        out_shape=[jax.ShapeDtypeStruct((t, d), F32),
                   jax.ShapeDtypeStruct((n_win * cap, d), F32),
                   jax.ShapeDtypeStruct((n_win * 8, tm), I32),
                   jax.ShapeDtypeStruct((n_win * 8, tm), F32),
                   jax.ShapeDtypeStruct((n_win * N_EXPERTS, LANES), I32)],
        grid=(n_win,),
        in_specs=[pl.BlockSpec((tm, d), row), pl.BlockSpec((tm, A_WIDTH), row),
                  pl.BlockSpec((tm, A_WIDTH), row), pl.BlockSpec((tm, A_WIDTH), row),
                  _const_spec((1, LANES)), _const_spec((1, LANES)), _const_spec(w_out_bf.shape),
                  _const_spec((1, d)), _const_spec(w_xq_bf.shape), _const_spec((1, xq_g.shape[-1])),
                  pl.BlockSpec((1, n_mem, d), mem), pl.BlockSpec((1, n_mem, d), mem),
                  _const_spec(w_xo_bf.shape),
                  _const_spec((1, d)), _const_spec((N_EXPERTS, d)), _const_spec((N_EXPERTS, 1))],
        out_specs=[pl.BlockSpec((tm, d), row), pl.BlockSpec((cap, d), row),
                   pl.BlockSpec((8, tm), row), pl.BlockSpec((8, tm), row),
                   pl.BlockSpec((N_EXPERTS, LANES), row)],
        compiler_params=_cparams(("parallel",)),
        name="post",
    )(x2d, oa, ob, gt, vec(subln_g), vec(gn_g), w_out_bf, vec(norm_x_g), w_xq_bf, vec(xq_g),
      mk_bf, mv_bf, w_xo_bf, vec(norm_ffn_g), w_router.T, b_router.reshape(N_EXPERTS, 1))


def _moe_expert_kernel(te_ref, tf_ref, tv_ref, g0_ref, gn_ref, sd_ref, xs_hbm,
                       wgu_ref, bgu_ref, wd_ref, bd_ref, ys_hbm,
                       xbuf, ybuf, wgu_bf, wd_bf, gsem, ssem, *, tm):
    i = pl.program_id(0)
    last = pl.num_programs(0) - 1
    slot = i % 2
    nb = tm // ROW_BLOCK
    valid = tv_ref[i] == 1
    next_valid = jnp.logical_and(i < last, tv_ref[jnp.minimum(i + 1, last)] == 1)

    def gather(tab_ref, s):
        for j in range(nb):
            r = pl.multiple_of(tab_ref[0, 0, j], ROW_BLOCK)
            pltpu.make_async_copy(xs_hbm.at[pl.ds(r, ROW_BLOCK)],
                                  xbuf.at[s, pl.ds(j * ROW_BLOCK, ROW_BLOCK)], gsem.at[s]).start()

    def scatter(s):
        for j in range(nb):
            r = pl.multiple_of(sd_ref[0, 0, j], ROW_BLOCK)
            pltpu.make_async_copy(ybuf.at[s, pl.ds(j * ROW_BLOCK, ROW_BLOCK)],
                                  ys_hbm.at[pl.ds(r, ROW_BLOCK)], ssem.at[s]).start()

    def wait_gather(s):
        pltpu.make_async_copy(xs_hbm.at[pl.ds(0, tm)], xbuf.at[s], gsem.at[s]).wait()

    def wait_scatter(s):
        pltpu.make_async_copy(ybuf.at[s], ys_hbm.at[pl.ds(0, tm)], ssem.at[s]).wait()

    @pl.when(jnp.logical_and(i == 0, valid))
    def _():
        gather(g0_ref, 0)

    @pl.when(jnp.logical_and(valid, tf_ref[i] == 1))
    def _():
        wgu_bf[...] = wgu_ref[0].astype(BF16)
        wd_bf[...] = wd_ref[0].astype(BF16)

    @pl.when(valid)
    def _():
        wait_gather(slot)

        @pl.when(next_valid)
        def _():
            gather(gn_ref, 1 - slot)

        de = wd_bf.shape[0]
        gu = _dot(xbuf[slot].astype(BF16), wgu_bf[...]) + bgu_ref[0]
        gate = jnp.minimum(gu[:, :de], SWIGLU_LIMIT)
        up = jnp.clip(gu[:, de:], -SWIGLU_LIMIT, SWIGLU_LIMIT)
        act = (up + 1.0) * gate * _sigmoid(SWIGLU_ALPHA * gate)
        ybuf[slot] = _dot(act.astype(BF16), wd_bf[...]) + bd_ref[0]

        @pl.when(i > 0)
        def _():
            wait_scatter(1 - slot)

        scatter(slot)

        @pl.when(jnp.logical_not(next_valid))
        def _():
            wait_scatter(slot)


def _moe_expert(tables, xs, w_gate_up, b_gate_up, w_down, b_down, tm):
    tile_expert, tile_first, tile_valid, gsrc, sdst = tables
    nt = tile_expert.shape[0]
    nb = tm // ROW_BLOCK
    d = xs.shape[1]
    ne, _, dgu = w_gate_up.shape
    de = w_down.shape[1]
    smem = lambda f: pl.BlockSpec((1, 1, nb), f, memory_space=pltpu.SMEM)
    wmap = lambda i, te, tf, tv: (te[i], 0, 0)
    grid_spec = pltpu.PrefetchScalarGridSpec(
        num_scalar_prefetch=3,
        grid=(nt,),
        in_specs=[smem(lambda i, te, tf, tv: (0, 0, 0)),
                  smem(lambda i, te, tf, tv: (jnp.minimum(i + 1, nt - 1), 0, 0)),
                  smem(lambda i, te, tf, tv: (i, 0, 0)),
                  pl.BlockSpec(memory_space=pl.ANY),
                  pl.BlockSpec((1, d, dgu), wmap), pl.BlockSpec((1, 1, dgu), wmap),
                  pl.BlockSpec((1, de, d), wmap), pl.BlockSpec((1, 1, d), wmap)],
        out_specs=pl.BlockSpec(memory_space=pl.ANY),
        scratch_shapes=[pltpu.VMEM((2, tm, d), F32), pltpu.VMEM((2, tm, d), F32),
                        pltpu.VMEM((d, dgu), BF16), pltpu.VMEM((de, d), BF16),
                        pltpu.SemaphoreType.DMA((2,)), pltpu.SemaphoreType.DMA((2,))],
    )
    return pl.pallas_call(
        functools.partial(_moe_expert_kernel, tm=tm),
        out_shape=jax.ShapeDtypeStruct(xs.shape, F32),
        grid_spec=grid_spec,
        input_output_aliases={6: 0},
        compiler_params=_cparams(("arbitrary",)),
        name="moe_expert",
    )(tile_expert, tile_first, tile_valid, gsrc.reshape(nt, 1, nb), gsrc.reshape(nt, 1, nb),
      sdst.reshape(nt, 1, nb), xs, w_gate_up, b_gate_up.reshape(ne, 1, dgu), w_down,
      b_down.reshape(ne, 1, d))


def _moe_combine_kernel(x_ref, ys_ref, rloc_ref, gate_ref, o_ref):
    cap = ys_ref.shape[0]
    w = x_ref.shape[0]
    riota = lax.broadcasted_iota(I32, (cap, w), 0)
    gt = jnp.zeros((cap, w), F32)
    for k in range(TOP_K):
        gt = jnp.where(riota == rloc_ref[k:k + 1, :], gate_ref[k:k + 1, :], gt)
    o_ref[...] = x_ref[...] + _dot_tn(gt.astype(BF16), ys_ref[...].astype(BF16))


def _moe_combine(x2, ys, rloc, gate, tm):
    t, d = x2.shape
    cap = _window_cap(tm)
    row = lambda i: (i, 0)
    return pl.pallas_call(
        _moe_combine_kernel,
        out_shape=jax.ShapeDtypeStruct((t, d), F32),
        grid=(t // tm,),
        in_specs=[pl.BlockSpec((tm, d), row), pl.BlockSpec((cap, d), row),
                  pl.BlockSpec((8, tm), row), pl.BlockSpec((8, tm), row)],
        out_specs=pl.BlockSpec((tm, d), row),
        compiler_params=_cparams(("parallel",)),
        name="moe_combine",
    )(x2, ys, rloc, gate)


def _moe_tables(cnt, cap, n_picks, tm_e):
    n_win = cnt.shape[0]
    bpt = tm_e // ROW_BLOCK
    nt = (n_picks // ROW_BLOCK + n_win * N_EXPERTS) // bpt + N_EXPERTS
    blk = (cnt + ROW_BLOCK - 1) // ROW_BLOCK
    seg_blk = jnp.cumsum(blk, axis=1) - blk
    cum_incl = jnp.cumsum(blk, axis=0)
    cum_excl = cum_incl - blk
    tot = cum_incl[-1]
    tiles = (tot + bpt - 1) // bpt
    tile_end = jnp.cumsum(tiles)
    tile_beg = tile_end - tiles
    ti = jnp.arange(nt, dtype=I32)
    te = jnp.minimum(jnp.sum((ti[:, None] >= tile_end[None, :]).astype(I32), axis=1), N_EXPERTS - 1)
    tv = ti < tile_end[-1]
    tf = jnp.concatenate([jnp.ones((1,), bool), te[1:] != te[:-1]])
    v = (ti - tile_beg[te])[:, None] * bpt + jnp.arange(bpt, dtype=I32)[None, :]
    ok = tv[:, None] & (v < tot[te][:, None])
    widx = jnp.sum((cum_incl.T[te][:, None, :] <= v[:, :, None]).astype(I32), axis=-1)
    onehot = jnp.minimum(widx, n_win - 1)[:, :, None] == jnp.arange(n_win, dtype=I32)
    base = (jnp.arange(n_win, dtype=I32) * cap)[None, :] + (seg_blk.T[te] - cum_excl.T[te]) * ROW_BLOCK
    row = jnp.sum(jnp.where(onehot, base[:, None, :], 0), axis=-1) + v * ROW_BLOCK
    assert n_win >= bpt
    spare = jnp.arange(bpt, dtype=I32)[None, :] * cap + (cap - 2 * ROW_BLOCK)
    gsrc = jnp.where(ok, row, cap - ROW_BLOCK).astype(I32)
    sdst = jnp.where(ok, row, spare).astype(I32)
    return te.astype(I32), tf.astype(I32), tv.astype(I32), gsrc, sdst


def _moe(x2, xs, rloc, gate, cnt, w_gate_up, b_gate_up, w_down, b_down, tm_w, tm_e):
    n_win = x2.shape[0] // tm_w
    cnt2 = cnt.reshape(n_win, N_EXPERTS, LANES)[:, :, 0]
    tables = _moe_tables(cnt2, _window_cap(tm_w), x2.shape[0] * TOP_K, tm_e)
    ys = _moe_expert(tables, xs, w_gate_up, b_gate_up, w_down, b_down, tm_e)
    return _moe_combine(x2, ys, rloc, gate, tm_w)


def _pick(n, pref):
    return pref if n % pref == 0 else n


def _group(x, mk_bf, mv_bf, attn_fn, attn_tk, s0, l, p, bf, lam0):
    b, t, d = x.shape
    x2d = x.reshape(b * t, d)
    (qs, kb, vb, k32, v32, hq, hk, hv, lf, gt) = _mixer_in(
        x2d, p["norm_mix_g"][l], bf["w_in"][l], p["a_qnorm_g"][l], p["a_knorm_g"][l],
        p["hgrn_lb_logits"], l, _pick(b * t, 512), attn_tk)
    r3 = lambda a: a.reshape(b, t, -1)
    oa = attn_fn(r3(qs), r3(kb), vb if attn_tk else r3(vb))
    ob, s_new = _hgrn(r3(hq), r3(hk), r3(hv), r3(lf), s0, _pick(t, 128))
    tm_w = _pick(t, 256)
    tm_e = 512 if b * t >= 4096 else 64
    x2, xs, rloc, gate, cnt = _post(
        x2d, oa.reshape(b * t, -1), ob.reshape(b * t, -1), gt, p["a_subln_g"][l], p["b_gnorm_g"][l],
        bf["w_out"][l], p["norm_x_g"][l], bf["w_xq"][l], p["xq_norm_g"][l], mk_bf, mv_bf,
        bf["w_xo"][l], p["norm_ffn_g"][l], p["w_router"][l], p["b_router"][l], lam0,
        tm_w, t)
    y = _moe(x2, xs, rloc, gate, cnt, p["w_gate_up"][l], p["b_gate_up"][l], p["w_down"][l],
             p["b_down"][l], tm_w, tm_e)
    return (y.reshape(b, t, d), k32.reshape(b, t, 2 * A_HEADS, A_HEAD_DIM),
            v32.reshape(b, t, A_HEADS, 2 * A_HEAD_DIM), s_new)


def kernel(x_prompt, x_sample, mem_prompt, cache_attn_k, cache_attn_v, state_hgrn, cache_mem_k, cache_mem_v, norm_mix_g, w_in, a_qnorm_g, a_knorm_g, lambda_q1, lambda_k1, lambda_q2, lambda_k2, a_subln_g, hgrn_lb_logits, b_gnorm_g, w_out, norm_x_g, norm_mem_g, w_xq, w_mk, w_mv, xq_norm_g, xk_norm_g, w_xo, norm_ffn_g, w_router, b_router, w_gate_up, b_gate_up, w_down, b_down):
    p = dict(norm_mix_g=norm_mix_g, a_qnorm_g=a_qnorm_g, a_knorm_g=a_knorm_g, a_subln_g=a_subln_g,
             hgrn_lb_logits=hgrn_lb_logits, b_gnorm_g=b_gnorm_g, norm_x_g=norm_x_g,
             xq_norm_g=xq_norm_g, norm_ffn_g=norm_ffn_g, w_router=w_router, b_router=b_router,
             w_gate_up=w_gate_up, b_gate_up=b_gate_up, w_down=w_down, b_down=b_down)
    bf = dict(w_in=w_in.astype(BF16), w_out=w_out.astype(BF16), w_xq=w_xq.astype(BF16),
              w_xo=w_xo.astype(BF16))
    depth = w_in.shape[0]
    b, s, d = x_prompt.shape
    db, ds, _ = x_sample.shape
    n_mem = mem_prompt.shape[1]
    xp, xs = x_prompt, x_sample
    outs = [[] for _ in range(8)]
    for l in range(depth):
        lam0 = 0.8 - 0.6 * math.exp(-0.3 * l)
        lam_vec = jnp.stack([lambda_q1[l], lambda_k1[l], lambda_q2[l], lambda_k2[l]])

        mk, mv = _mem_kv(mem_prompt.reshape(b * n_mem, d), norm_mem_g[l], w_mk[l].astype(BF16),
                         w_mv[l].astype(BF16), xk_norm_g[l], _pick(b * n_mem, 512))
        tk = 2 * LANES
        attn_p = lambda q, k, vt: _attn_prompt(lam_vec, q, k, vt, lam0, 2 * tk, tk)
        s0 = jnp.zeros((b, B_HEADS, B_DIM, B_DIM), F32)
        xp, kp, vp, sp = _group(xp, mk.reshape(b, n_mem, d).astype(BF16),
                                mv.reshape(b, n_mem, d).astype(BF16), attn_p, tk, s0, l, p, bf, lam0)

        kc = cache_attn_k[l].reshape(db, -1, A_WIDTH).astype(BF16)
        vc = cache_attn_v[l].reshape(db, -1, A_WIDTH).astype(BF16)
        attn_s = lambda q, k, v: _attn_sample(lam_vec, q, kc, vc, k, v, lam0)
        xs, ks, vs, ss = _group(xs, cache_mem_k[l].reshape(db, n_mem, d).astype(BF16),
                                cache_mem_v[l].reshape(db, n_mem, d).astype(BF16), attn_s, None,
                                state_hgrn[l], l, p, bf, lam0)
        for lst, val in zip(outs, (kp, vp, sp, mk.reshape(b, n_mem, X_HEADS, -1),
                                   mv.reshape(b, n_mem, X_HEADS, -1), ks, vs, ss)):
            lst.append(val)
    return (xp, xs) + tuple(jnp.stack(o) for o in outs)
```

```python
import functools
import math

import jax
import jax.numpy as jnp
from jax import lax
from jax.experimental import pallas as pl
from jax.experimental.pallas import tpu as pltpu

F32 = jnp.float32
BF16 = jnp.bfloat16
I32 = jnp.int32

EPS = 1e-6
CHUNK = 64
A_HEADS = 4
A_HEAD_DIM = 64
A_WIDTH = A_HEADS * 2 * A_HEAD_DIM
B_HEADS = 4
B_DIM = 128
X_HEADS = 4
N_EXPERTS = 32
TOP_K = 4
SWIGLU_LIMIT = 7.0
SWIGLU_ALPHA = 1.702

LANES = 128
ROW_BLOCK = 8
VMEM_LIMIT = 56 * 1024 * 1024


def _cparams(sem):
    return pltpu.CompilerParams(dimension_semantics=sem, vmem_limit_bytes=VMEM_LIMIT)


def _const_spec(shape):
    zeros = (0,) * len(shape)
    return pl.BlockSpec(shape, lambda *_: zeros)


def _rms(x, g):
    return x * lax.rsqrt(jnp.mean(x * x, axis=-1, keepdims=True) + EPS) * g


def _dot(a, b):
    return jnp.dot(a, b, preferred_element_type=F32)


def _dot_nt(a, b):
    return lax.dot_general(a, b, (((1,), (1,)), ((), ())), preferred_element_type=F32)


def _dot_tn(a, b):
    return lax.dot_general(a, b, (((0,), (0,)), ((), ())), preferred_element_type=F32)


def _sigmoid(x):
    return 1.0 / (1.0 + jnp.exp(-x))


def _mixer_in_kernel(x_ref, g_ref, w_ref, gq_ref, gk_ref, lbl_ref, seg_ref,
                     qs_ref, kb_ref, vb_ref, k32_ref, v32_ref,
                     hq_ref, hk_ref, hv_ref, lf_ref, gt_ref, *, layer, tk):
    hb = _rms(x_ref[...], g_ref[...]).astype(BF16)

    def proj(j):
        return _dot(hb, w_ref[:, j * A_WIDTH:(j + 1) * A_WIDTH])

    seg = seg_ref[...]

    def head_norm(a, g):
        ss = _dot((a * a).astype(BF16), seg)
        return a * lax.rsqrt(ss * (1.0 / A_HEAD_DIM) + EPS) * g

    q = head_norm(proj(0), gq_ref[...])
    qs_ref[...] = (q * (A_HEAD_DIM ** -0.5 * math.log2(math.e))).astype(BF16)
    k = head_norm(proj(1), gk_ref[...])
    for hh in range(2 * A_HEADS):
        k32_ref[:, hh, :] = k[:, hh * A_HEAD_DIM:(hh + 1) * A_HEAD_DIM]
    kb_ref[...] = k.astype(BF16)
    v = proj(2)
    for hh in range(A_HEADS):
        v32_ref[:, hh, :] = v[:, hh * LANES:(hh + 1) * LANES]
    if tk is None:
        vb_ref[...] = v.astype(BF16)
    else:
        vt = v.T.astype(BF16)
        for c in range(vb_ref.shape[0]):
            vb_ref[c] = vt[:, c * tk:(c + 1) * tk]

    lbl = lbl_ref[...]
    e = jnp.exp(lbl - jnp.max(lbl, axis=0, keepdims=True))
    lb = jnp.sum(e[:layer + 1], axis=0, keepdims=True) / jnp.sum(e, axis=0, keepdims=True)

    qb = proj(3)
    hq_ref[...] = (qb * _sigmoid(qb)).astype(BF16)
    f = lb + (1.0 - lb) * _sigmoid(proj(4))
    hk_ref[...] = (1.0 - f).astype(BF16)
    lf_ref[...] = jnp.log(f)
    hv_ref[...] = proj(5).astype(BF16)
    gb = proj(6)
    gt_ref[...] = (gb * _sigmoid(gb)).astype(BF16)


def _mixer_in(x2d, norm_g, w_in_bf, gq, gk, lb_logits, layer, tm, tk):
    t, d = x2d.shape
    w = A_WIDTH
    seg = (jnp.arange(w)[:, None] // A_HEAD_DIM == jnp.arange(w)[None, :] // A_HEAD_DIM).astype(BF16)
    row = lambda i: (i, 0)
    out_dt = [BF16, BF16, BF16, F32, F32, BF16, BF16, BF16, F32, BF16]
    out_shape = [jax.ShapeDtypeStruct((t, w), dt) for dt in out_dt]
    out_specs = [pl.BlockSpec((tm, w), row) for _ in out_dt]
    row3 = lambda i: (i, 0, 0)
    out_shape[3] = jax.ShapeDtypeStruct((t, 2 * A_HEADS, A_HEAD_DIM), F32)
    out_specs[3] = pl.BlockSpec((tm, 2 * A_HEADS, A_HEAD_DIM), row3)
    out_shape[4] = jax.ShapeDtypeStruct((t, A_HEADS, 2 * A_HEAD_DIM), F32)
    out_specs[4] = pl.BlockSpec((tm, A_HEADS, 2 * A_HEAD_DIM), row3)
    if tk is not None:
        out_shape[2] = jax.ShapeDtypeStruct((t // tk, w, tk), BF16)
        out_specs[2] = pl.BlockSpec((tm // tk, w, tk), lambda i: (i, 0, 0))
    return pl.pallas_call(
        functools.partial(_mixer_in_kernel, layer=layer, tk=tk),
        out_shape=out_shape,
        grid=(t // tm,),
        in_specs=[pl.BlockSpec((tm, d), row), _const_spec((1, d)), _const_spec(w_in_bf.shape),
                  _const_spec((1, w)), _const_spec((1, w)),
                  _const_spec(lb_logits.shape), _const_spec((w, w))],
        out_specs=out_specs,
        compiler_params=_cparams(("parallel",)),
        name="mixer_in",
    )(x2d, norm_g.reshape(1, d), w_in_bf,
      jnp.tile(gq, 2 * A_HEADS).reshape(1, w), jnp.tile(gk, 2 * A_HEADS).reshape(1, w),
      lb_logits, seg)


def _lambda(lam_ref, lam0):
    lv = lam_ref[...]
    e1 = jnp.exp(jnp.sum(lv[0:1] * lv[1:2], axis=-1, keepdims=True))
    e2 = jnp.exp(jnp.sum(lv[2:3] * lv[3:4], axis=-1, keepdims=True))
    return e1 - e2 + lam0


def _split_maps(k):
    first = lax.broadcasted_iota(I32, k.shape, 1) < A_HEAD_DIM
    zero = jnp.zeros_like(k)
    return jnp.where(first, k, zero), jnp.where(first, zero, k)


def _attn_prompt_kernel(lam_ref, bias_ref, q_ref, k_ref, vt_ref, o_ref,
                        sa_ref, sb_ref, pa_ref, pb_ref, acc_ref, *, tq, tk, lam0):
    assert tq == 2 * tk
    qi = pl.program_id(2)
    q = q_ref[0]
    s_bufs = (sa_ref, sb_ref)
    p_bufs = (pa_ref, pb_ref)

    def phase(j, par, mx_cur, ml, masked, prefetch):
        s_ref, p_ref = s_bufs[par], p_bufs[par]
        sn_ref, pp_ref = s_bufs[1 - par], p_bufs[1 - par]
        vt = vt_ref[jnp.maximum(j - 1, 0)]
        mx_next = []
        if prefetch:
            k = k_ref[0, pl.ds(pl.multiple_of((j + 1) * tk, tk), tk), :]
            for mp, km in enumerate(_split_maps(k)):
                s = _dot_nt(km, q)
                sn_ref[mp] = s
                mx_next.append(jnp.max(s, axis=0, keepdims=True))
        new_ml = []
        for mp in range(2):
            m, l = ml[mp]
            pv = _dot(vt, pp_ref[mp])
            s = s_ref[mp]
            s_max = mx_cur[mp]
            if masked is not None:
                s = s + bias_ref[masked]
                s_max = jnp.max(s, axis=0, keepdims=True)
            m_new = jnp.maximum(m, s_max)
            alpha = jnp.exp2(m - m_new)
            p = jnp.exp2(s - m_new)
            p_ref[mp] = p.astype(BF16)
            acc_ref[mp] = alpha * (acc_ref[mp] + pv)
            new_ml.append((m_new, alpha * l + jnp.sum(p, axis=0, keepdims=True)))
        return tuple(mx_next), tuple(new_ml)

    k0 = k_ref[0, pl.ds(0, tk), :]
    mx = []
    for mp, km in enumerate(_split_maps(k0)):
        s = _dot_nt(km, q)
        sa_ref[mp] = s
        mx.append(jnp.max(s, axis=0, keepdims=True))
    pb_ref[...] = jnp.zeros_like(pb_ref)
    acc_ref[...] = jnp.zeros_like(acc_ref)
    ml = tuple((jnp.full((1, tq), -jnp.inf, F32), jnp.zeros((1, tq), F32)) for _ in range(2))

    def pair(t, carry):
        mx_a, ml = carry
        mx_b, ml = phase(2 * t, 0, mx_a, ml, None, True)
        mx_a, ml = phase(2 * t + 1, 1, mx_b, ml, None, True)
        return mx_a, ml

    mx_a, ml = lax.fori_loop(0, qi, pair, (tuple(mx), ml))
    mx_b, ml = phase(2 * qi, 0, mx_a, ml, 0, True)
    _, ml = phase(2 * qi + 1, 1, mx_b, ml, 1, False)
    vt = vt_ref[2 * qi + 1]
    (_, l1), (_, l2) = ml
    o1 = (acc_ref[0] + _dot(vt, pb_ref[0])) / l1
    o2 = (acc_ref[1] + _dot(vt, pb_ref[1])) / l2
    o_ref[0] = (o1 - _lambda(lam_ref, lam0) * o2).T.astype(o_ref.dtype)


def _attn_prompt(lam_vec, qs, kb, vt, lam0, tq, tk):
    b, s, w = qs.shape
    nkb = s // tk
    blk = lambda bi, hi, qi: (bi, qi, hi)
    kc = (jnp.arange(2 * tk) // CHUNK).reshape(2, tk, 1)
    qc = (jnp.arange(tq) // CHUNK).reshape(1, 1, tq)
    bias = jnp.where(kc <= qc, 0.0, -jnp.inf).astype(F32)
    return pl.pallas_call(
        functools.partial(_attn_prompt_kernel, tq=tq, tk=tk, lam0=lam0),
        out_shape=jax.ShapeDtypeStruct((b, s, w), F32),
        grid=(b, A_HEADS, s // tq),
        in_specs=[_const_spec(lam_vec.shape), _const_spec(bias.shape),
                  pl.BlockSpec((1, tq, LANES), blk),
                  pl.BlockSpec((1, s, LANES), lambda bi, hi, qi: (bi, 0, hi)),
                  pl.BlockSpec((nkb, LANES, tk), lambda bi, hi, qi: (bi, hi, 0))],
        out_specs=pl.BlockSpec((1, tq, LANES), blk),
        scratch_shapes=[pltpu.VMEM((2, tk, tq), F32), pltpu.VMEM((2, tk, tq), F32),
                        pltpu.VMEM((2, tk, tq), BF16), pltpu.VMEM((2, tk, tq), BF16),
                        pltpu.VMEM((2, LANES, tq), F32)],
        compiler_params=_cparams(("parallel", "parallel", "arbitrary")),
        name="attn_prompt",
    )(lam_vec, bias, qs, kb, vt)


def _attn_sample_kernel(lam_ref, q_ref, kc_ref, vc_ref, kn_ref, vn_ref, o_ref, *, lam0):
    q = q_ref[0]
    vc = vc_ref[0]
    vn = vn_ref[0]
    outs = []
    for kcm, knm in zip(_split_maps(kc_ref[0]), _split_maps(kn_ref[0])):
        sc = _dot_nt(q, kcm)
        sn = _dot_nt(q, knm)
        m = jnp.maximum(jnp.max(sc, axis=-1, keepdims=True), jnp.max(sn, axis=-1, keepdims=True))
        pc = jnp.exp2(sc - m)
        pn = jnp.exp2(sn - m)
        l = jnp.sum(pc, axis=-1, keepdims=True) + jnp.sum(pn, axis=-1, keepdims=True)
        outs.append((_dot(pc.astype(BF16), vc) + _dot(pn.astype(BF16), vn)) / l)
    o_ref[0] = (outs[0] - _lambda(lam_ref, lam0) * outs[1]).astype(o_ref.dtype)


def _attn_sample(lam_vec, qs, kc, vc, kn, vn, lam0):
    b, t, w = qs.shape
    past = kc.shape[1]
    blk = lambda bi, hi: (bi, 0, hi)
    return pl.pallas_call(
        functools.partial(_attn_sample_kernel, lam0=lam0),
        out_shape=jax.ShapeDtypeStruct((b, t, w), F32),
        grid=(b, A_HEADS),
        in_specs=[_const_spec(lam_vec.shape), pl.BlockSpec((1, t, LANES), blk),
                  pl.BlockSpec((1, past, LANES), blk), pl.BlockSpec((1, past, LANES), blk),
                  pl.BlockSpec((1, t, LANES), blk), pl.BlockSpec((1, t, LANES), blk)],
        out_specs=pl.BlockSpec((1, t, LANES), blk),
        compiler_params=_cparams(("parallel", "parallel")),
        name="attn_sample",
    )(lam_vec, qs, kc, vc, kn, vn)


def _cumsum_rows(x):
    c = x.shape[0]
    row = lax.broadcasted_iota(I32, x.shape, 0)
    d = 1
    while d < c:
        x = x + jnp.where(row >= d, pltpu.roll(x, d, axis=0), 0.0)
        d *= 2
    return x


def _block_ref_rows(cum, m):
    c, n = cum.shape
    if m >= 8:
        r = cum.reshape(c // (2 * m), 2 * m, n)[:, m - 1:m, :]
        return jnp.broadcast_to(r, (c // (2 * m), 2 * m, n)).reshape(c, n)
    pos = lax.broadcasted_iota(I32, cum.shape, 0) % (2 * m)
    out = cum
    for p in range(2 * m):
        shift = p - (m - 1)
        if shift != 0:
            out = jnp.where(pos == p, pltpu.roll(cum, shift % c, axis=0), out)
    return out


def _hgrn_kernel(hq_ref, hk_ref, hv_ref, lf_ref, s0_ref, o_ref, st_ref, stt_ref, *, c):
    ci = pl.program_id(1)

    @pl.when(ci == 0)
    def _():
        for h in range(B_HEADS):
            stt_ref[h] = s0_ref[0, h].T

    row = lax.broadcasted_iota(I32, (c, B_DIM), 0)
    r2 = lax.broadcasted_iota(I32, (c, c), 0)
    c2 = lax.broadcasted_iota(I32, (c, c), 1)
    levels = []
    m = c // 2
    while m >= 1:
        levels.append((m, (row % (2 * m)) >= m, (r2 // (2 * m)) == (c2 // (2 * m))))
        m //= 2
    for h in range(B_HEADS):
        sl = slice(h * B_DIM, (h + 1) * B_DIM)
        hq = hq_ref[0, :, sl].astype(F32)
        hk = hk_ref[0, :, sl].astype(F32)
        hv = hv_ref[0, :, sl]
        cum = _cumsum_rows(lf_ref[0, :, sl])
        last = cum[c - 1:c, :]

        a = None
        for m, upper, same_block in levels:
            d = cum - _block_ref_rows(cum, m)
            qd = hq * jnp.exp(jnp.where(upper, d, -jnp.inf))
            kd = hk * jnp.exp(jnp.where(upper, -jnp.inf, -d))
            p = _dot_nt(qd.astype(BF16), kd.astype(BF16))
            a = jnp.where(same_block, p, 0.0 if a is None else a)

        stt = stt_ref[h]
        o = _dot_nt((hq * jnp.exp(cum)).astype(BF16), stt.astype(BF16))
        o = o + _dot(a.astype(BF16), hv)
        o = o + jnp.sum(hq * hk, axis=-1, keepdims=True) * hv.astype(F32)
        o_ref[0, :, sl] = o.astype(o_ref.dtype)

        kl = (hk * jnp.exp(last - cum)).astype(BF16)
        stt_ref[h] = stt * jnp.exp(last) + _dot_tn(hv, kl)

    @pl.when(ci == pl.num_programs(1) - 1)
    def _():
        for h in range(B_HEADS):
            st_ref[0, h] = stt_ref[h].T


def _hgrn(hq, hk, hv, lf, s0, c):
    b, s, w = hq.shape
    blk = lambda bi, ci: (bi, ci, 0)
    sblk = lambda bi, ci: (bi, 0, 0, 0)
    return pl.pallas_call(
        functools.partial(_hgrn_kernel, c=c),
        out_shape=[jax.ShapeDtypeStruct((b, s, w), F32),
                   jax.ShapeDtypeStruct((b, B_HEADS, B_DIM, B_DIM), F32)],
        grid=(b, s // c),
        in_specs=[pl.BlockSpec((1, c, w), blk)] * 4 + [pl.BlockSpec((1, B_HEADS, B_DIM, B_DIM), sblk)],
        out_specs=[pl.BlockSpec((1, c, w), blk), pl.BlockSpec((1, B_HEADS, B_DIM, B_DIM), sblk)],
        scratch_shapes=[pltpu.VMEM((B_HEADS, B_DIM, B_DIM), F32)],
        compiler_params=_cparams(("parallel", "arbitrary")),
        name="hgrn",
    )(hq, hk, hv, lf, s0)


def _mem_kv_kernel(m_ref, g_ref, wk_ref, wv_ref, gk_ref, k_ref, v_ref):
    hm = _rms(m_ref[...], g_ref[...]).astype(BF16)
    k = _dot(hm, wk_ref[...])
    gk = gk_ref[...]
    hd = k.shape[-1] // X_HEADS
    for h in range(X_HEADS):
        sl = slice(h * hd, (h + 1) * hd)
        k_ref[:, sl] = _rms(k[:, sl], gk)
    v_ref[...] = _dot(hm, wv_ref[...])


def _mem_kv(mem2d, norm_g, w_mk_bf, w_mv_bf, xk_g, tm):
    t, d = mem2d.shape
    row = lambda i: (i, 0)
    return pl.pallas_call(
        _mem_kv_kernel,
        out_shape=[jax.ShapeDtypeStruct((t, d), F32)] * 2,
        grid=(t // tm,),
        in_specs=[pl.BlockSpec((tm, d), row), _const_spec((1, d)), _const_spec(w_mk_bf.shape),
                  _const_spec(w_mv_bf.shape), _const_spec((1, xk_g.shape[-1]))],
        out_specs=[pl.BlockSpec((tm, d), row)] * 2,
        compiler_params=_cparams(("parallel",)),
        name="mem_kv",
    )(mem2d, norm_g.reshape(1, d), w_mk_bf, w_mv_bf, xk_g.reshape(1, -1))


def _post_kernel(x_ref, oa_ref, ob_ref, gt_ref, sg_ref, gg_ref, wo_ref,
                 nx_ref, wq_ref, xq_ref, mk_ref, mv_ref, wxo_ref,
                 nf_ref, wr_ref, br_ref,
                 x2_ref, xs_ref, rloc_ref, gate_ref, cnt_ref, *, lam0):
    ya = []
    yb = []
    for h in range(A_HEADS):
        sl = slice(h * LANES, (h + 1) * LANES)
        ya.append((_rms(oa_ref[:, sl], sg_ref[...]) * (1.0 - lam0)).astype(BF16))
        yb.append((_rms(ob_ref[:, sl], gg_ref[...]) * gt_ref[:, sl].astype(F32)).astype(BF16))
    y = jnp.concatenate(ya + yb, axis=-1)
    x1 = x_ref[...] + _dot(y, wo_ref[...])

    q = _dot(_rms(x1, nx_ref[...]).astype(BF16), wq_ref[...])
    hd = q.shape[-1] // X_HEADS
    os_ = []
    for h in range(X_HEADS):
        sl = slice(h * hd, (h + 1) * hd)
        qh = _rms(q[:, sl], xq_ref[...]).astype(BF16)
        s = _dot_nt(qh, mk_ref[0, :, sl]) * (hd ** -0.5)
        p = jnp.exp(s - jnp.max(s, axis=-1, keepdims=True))
        p = p / jnp.sum(p, axis=-1, keepdims=True)
        os_.append(_dot(p.astype(BF16), mv_ref[0, :, sl]).astype(BF16))
    x2 = x1 + _dot(jnp.concatenate(os_, axis=-1), wxo_ref[...])
    x2_ref[...] = x2

    h2 = _rms(x2, nf_ref[...])
    w = h2.shape[0]
    h_hi = h2.astype(BF16)
    h_lo = (h2 - h_hi.astype(F32)).astype(BF16)
    wr = wr_ref[...]
    w_hi = wr.astype(BF16)
    w_lo = (wr - w_hi.astype(F32)).astype(BF16)
    a = _dot_nt(jnp.concatenate([w_hi, w_lo], axis=0), h_hi)
    logits = a[:N_EXPERTS] + a[N_EXPERTS:] + _dot_nt(w_hi, h_lo) + br_ref[...]
    erow = lax.broadcasted_iota(I32, logits.shape, 0)
    picks, vals = [], []
    for k in range(TOP_K):
        mx = jnp.max(logits, axis=0, keepdims=True)
        ix = jnp.min(jnp.where(logits == mx, erow, N_EXPERTS), axis=0, keepdims=True)
        pick = erow == ix
        picks.append(pick)
        vals.append(mx)
        logits = jnp.where(pick, -jnp.inf, logits)
    ex = [jnp.exp(v - vals[0]) for v in vals]
    den = ex[0] + ex[1] + ex[2] + ex[3]

    onehot = jnp.where(picks[0] | picks[1] | picks[2] | picks[3], 1.0, 0.0)
    earlier = (lax.broadcasted_iota(I32, (w, w), 0) < lax.broadcasted_iota(I32, (w, w), 1))
    rank = _dot(onehot.astype(BF16), jnp.where(earlier, 1.0, 0.0).astype(BF16))
    cnt = jnp.sum(onehot, axis=1, keepdims=True).astype(I32)
    pcnt = ((cnt + (ROW_BLOCK - 1)) // ROW_BLOCK) * ROW_BLOCK
    pc_b = jnp.broadcast_to(pcnt.astype(F32), onehot.shape)
    seg_off = _cumsum_rows(pc_b) - pc_b
    cap = xs_ref.shape[0]
    riota = lax.broadcasted_iota(I32, (cap, w), 0)
    sel = None
    lane8 = lax.broadcasted_iota(I32, rloc_ref.shape, 0)
    rloc_out = jnp.zeros(rloc_ref.shape, I32)
    gate_out = jnp.zeros(gate_ref.shape, F32)
    for k in range(TOP_K):
        rloc = jnp.sum(jnp.where(picks[k], seg_off + rank, 0.0), axis=0, keepdims=True).astype(I32)
        hit = riota == rloc
        sel = hit if sel is None else (sel | hit)
        rloc_out = jnp.where(lane8 == k, rloc, rloc_out)
        gate_out = jnp.where(lane8 == k, ex[k] / den, gate_out)
    xs_ref[...] = _dot(jnp.where(sel, 1.0, 0.0).astype(BF16), h_hi)
    rloc_ref[...] = rloc_out
    gate_ref[...] = gate_out
    cnt_ref[...] = jnp.broadcast_to(cnt, cnt_ref.shape)


def _window_cap(w):
    cap = w * TOP_K + N_EXPERTS * (ROW_BLOCK - 1) + 2 * ROW_BLOCK
    return -(-cap // LANES) * LANES


def _post(x2d, oa, ob, gt, subln_g, gn_g, w_out_bf, norm_x_g, w_xq_bf, xq_g, mk_bf, mv_bf,
          w_xo_bf, norm_ffn_g, w_router, b_router, lam0, tm, rows_per_batch):
    t, d = x2d.shape
    n_win = t // tm
    cap = _window_cap(tm)
    row = lambda i: (i, 0)
    tiles_per_batch = rows_per_batch // tm
    mem = lambda i: (i // tiles_per_batch, 0, 0)
    n_mem = mk_bf.shape[1]
    vec = lambda a: a.reshape(1, -1)
    return pl.pallas_call(
        functools.partial(_post_kernel, lam0=lam0),
        out_shape=[jax.ShapeDtypeStruct((t, d), F32),
                   jax.ShapeDtypeStruct((n_win * cap, d), F32),
                   jax.ShapeDtypeStruct((n_win * 8, tm), I32),
                   jax.ShapeDtypeStruct((n_win * 8, tm), F32),
                   jax.ShapeDtypeStruct((n_win * N_EXPERTS, LANES), I32)],
        grid=(n_win,),
        in_specs=[pl.BlockSpec((tm, d), row), pl.BlockSpec((tm, A_WIDTH), row),
                  pl.BlockSpec((tm, A_WIDTH), row), pl.BlockSpec((tm, A_WIDTH), row),
                  _const_spec((1, LANES)), _const_spec((1, LANES)), _const_spec(w_out_bf.shape),
                  _const_spec((1, d)), _const_spec(w_xq_bf.shape), _const_spec((1, xq_g.shape[-1])),
                  pl.BlockSpec((1, n_mem, d), mem), pl.BlockSpec((1, n_mem, d), mem),
                  _const_spec(w_xo_bf.shape),
                  _const_spec((1, d)), _const_spec((N_EXPERTS, d)), _const_spec((N_EXPERTS, 1))],
        out_specs=[pl.BlockSpec((tm, d), row), pl.BlockSpec((cap, d), row),
                   pl.BlockSpec((8, tm), row), pl.BlockSpec((8, tm), row),
                   pl.BlockSpec((N_EXPERTS, LANES), row)],
        compiler_params=_cparams(("parallel",)),
        name="post",
    )(x2d, oa, ob, gt, vec(subln_g), vec(gn_g), w_out_bf, vec(norm_x_g), w_xq_bf, vec(xq_g),
      mk_bf, mv_bf, w_xo_bf, vec(norm_ffn_g), w_router.T, b_router.reshape(N_EXPERTS, 1))


def _moe_expert_kernel(te_ref, tf_ref, tv_ref, g0_ref, gn_ref, sd_ref, xs_hbm,
                       wgu_ref, bgu_ref, wd_ref, bd_ref, ys_hbm,
                       xbuf, ybuf, wgu_bf, wd_bf, gsem, ssem, *, tm):
    i = pl.program_id(0)
    last = pl.num_programs(0) - 1
    slot = i % 2
    nb = tm // ROW_BLOCK
    valid = tv_ref[i] == 1
    next_valid = jnp.logical_and(i < last, tv_ref[jnp.minimum(i + 1, last)] == 1)

    def gather(tab_ref, s):
        for j in range(nb):
            r = pl.multiple_of(tab_ref[0, 0, j], ROW_BLOCK)
            pltpu.make_async_copy(xs_hbm.at[pl.ds(r, ROW_BLOCK)],
                                  xbuf.at[s, pl.ds(j * ROW_BLOCK, ROW_BLOCK)], gsem.at[s]).start()

    def scatter(s):
        for j in range(nb):
            r = pl.multiple_of(sd_ref[0, 0, j], ROW_BLOCK)
            pltpu.make_async_copy(ybuf.at[s, pl.ds(j * ROW_BLOCK, ROW_BLOCK)],
                                  ys_hbm.at[pl.ds(r, ROW_BLOCK)], ssem.at[s]).start()

    def wait_gather(s):
        pltpu.make_async_copy(xs_hbm.at[pl.ds(0, tm)], xbuf.at[s], gsem.at[s]).wait()

    def wait_scatter(s):
        pltpu.make_async_copy(ybuf.at[s], ys_hbm.at[pl.ds(0, tm)], ssem.at[s]).wait()

    @pl.when(jnp.logical_and(i == 0, valid))
    def _():
        gather(g0_ref, 0)

    @pl.when(jnp.logical_and(valid, tf_ref[i] == 1))
    def _():
        wgu_bf[...] = wgu_ref[0].astype(BF16)
        wd_bf[...] = wd_ref[0].astype(BF16)

    @pl.when(valid)
    def _():
        wait_gather(slot)

        @pl.when(next_valid)
        def _():
            gather(gn_ref, 1 - slot)

        de = wd_bf.shape[0]
        gu = _dot(xbuf[slot].astype(BF16), wgu_bf[...]) + bgu_ref[0]
        gate = jnp.minimum(gu[:, :de], SWIGLU_LIMIT)
        up = jnp.clip(gu[:, de:], -SWIGLU_LIMIT, SWIGLU_LIMIT)
        act = (up + 1.0) * gate * _sigmoid(SWIGLU_ALPHA * gate)
        ybuf[slot] = _dot(act.astype(BF16), wd_bf[...]) + bd_ref[0]

        @pl.when(i > 0)
        def _():
            wait_scatter(1 - slot)

        scatter(slot)

        @pl.when(jnp.logical_not(next_valid))
        def _():
            wait_scatter(slot)


def _moe_expert(tables, xs, w_gate_up, b_gate_up, w_down, b_down, tm):
    tile_expert, tile_first, tile_valid, gsrc, sdst = tables
    nt = tile_expert.shape[0]
    nb = tm // ROW_BLOCK
    d = xs.shape[1]
    ne, _, dgu = w_gate_up.shape
    de = w_down.shape[1]
    smem = lambda f: pl.BlockSpec((1, 1, nb), f, memory_space=pltpu.SMEM)
    wmap = lambda i, te, tf, tv: (te[i], 0, 0)
    grid_spec = pltpu.PrefetchScalarGridSpec(
        num_scalar_prefetch=3,
        grid=(nt,),
        in_specs=[smem(lambda i, te, tf, tv: (0, 0, 0)),
                  smem(lambda i, te, tf, tv: (jnp.minimum(i + 1, nt - 1), 0, 0)),
                  smem(lambda i, te, tf, tv: (i, 0, 0)),
                  pl.BlockSpec(memory_space=pl.ANY),
                  pl.BlockSpec((1, d, dgu), wmap), pl.BlockSpec((1, 1, dgu), wmap),
                  pl.BlockSpec((1, de, d), wmap), pl.BlockSpec((1, 1, d), wmap)],
        out_specs=pl.BlockSpec(memory_space=pl.ANY),
        scratch_shapes=[pltpu.VMEM((2, tm, d), F32), pltpu.VMEM((2, tm, d), F32),
                        pltpu.VMEM((d, dgu), BF16), pltpu.VMEM((de, d), BF16),
                        pltpu.SemaphoreType.DMA((2,)), pltpu.SemaphoreType.DMA((2,))],
    )
    return pl.pallas_call(
        functools.partial(_moe_expert_kernel, tm=tm),
        out_shape=jax.ShapeDtypeStruct(xs.shape, F32),
        grid_spec=grid_spec,
        input_output_aliases={6: 0},
        compiler_params=_cparams(("arbitrary",)),
        name="moe_expert",
    )(tile_expert, tile_first, tile_valid, gsrc.reshape(nt, 1, nb), gsrc.reshape(nt, 1, nb),
      sdst.reshape(nt, 1, nb), xs, w_gate_up, b_gate_up.reshape(ne, 1, dgu), w_down,
      b_down.reshape(ne, 1, d))


def _moe_combine_kernel(x_ref, ys_ref, rloc_ref, gate_ref, o_ref):
    cap = ys_ref.shape[0]
    w = x_ref.shape[0]
    riota = lax.broadcasted_iota(I32, (cap, w), 0)
    gt = jnp.zeros((cap, w), F32)
    for k in range(TOP_K):
        gt = jnp.where(riota == rloc_ref[k:k + 1, :], gate_ref[k:k + 1, :], gt)
    o_ref[...] = x_ref[...] + _dot_tn(gt.astype(BF16), ys_ref[...].astype(BF16))


def _moe_combine(x2, ys, rloc, gate, tm):
    t, d = x2.shape
    cap = _window_cap(tm)
    row = lambda i: (i, 0)
    return pl.pallas_call(
        _moe_combine_kernel,
        out_shape=jax.ShapeDtypeStruct((t, d), F32),
        grid=(t // tm,),
        in_specs=[pl.BlockSpec((tm, d), row), pl.BlockSpec((cap, d), row),
                  pl.BlockSpec((8, tm), row), pl.BlockSpec((8, tm), row)],
        out_specs=pl.BlockSpec((tm, d), row),
        compiler_params=_cparams(("parallel",)),
        name="moe_combine",
    )(x2, ys, rloc, gate)


def _moe_tables(cnt, cap, n_picks, tm_e):
    n_win = cnt.shape[0]
    bpt = tm_e // ROW_BLOCK
    nt = (n_picks // ROW_BLOCK + n_win * N_EXPERTS) // bpt + N_EXPERTS
    blk = (cnt + ROW_BLOCK - 1) // ROW_BLOCK
    seg_blk = jnp.cumsum(blk, axis=1) - blk
    cum_incl = jnp.cumsum(blk, axis=0)
    cum_excl = cum_incl - blk
    tot = cum_incl[-1]
    tiles = (tot + bpt - 1) // bpt
    tile_end = jnp.cumsum(tiles)
    tile_beg = tile_end - tiles
    ti = jnp.arange(nt, dtype=I32)
    te = jnp.minimum(jnp.sum((ti[:, None] >= tile_end[None, :]).astype(I32), axis=1), N_EXPERTS - 1)
    tv = ti < tile_end[-1]
    tf = jnp.concatenate([jnp.ones((1,), bool), te[1:] != te[:-1]])
    v = (ti - tile_beg[te])[:, None] * bpt + jnp.arange(bpt, dtype=I32)[None, :]
    ok = tv[:, None] & (v < tot[te][:, None])
    widx = jnp.sum((cum_incl.T[te][:, None, :] <= v[:, :, None]).astype(I32), axis=-1)
    onehot = jnp.minimum(widx, n_win - 1)[:, :, None] == jnp.arange(n_win, dtype=I32)
    base = (jnp.arange(n_win, dtype=I32) * cap)[None, :] + (seg_blk.T[te] - cum_excl.T[te]) * ROW_BLOCK
    row = jnp.sum(jnp.where(onehot, base[:, None, :], 0), axis=-1) + v * ROW_BLOCK
    assert n_win >= bpt
    spare = jnp.arange(bpt, dtype=I32)[None, :] * cap + (cap - 2 * ROW_BLOCK)
    gsrc = jnp.where(ok, row, cap - ROW_BLOCK).astype(I32)
    sdst = jnp.where(ok, row, spare).astype(I32)
    return te.astype(I32), tf.astype(I32), tv.astype(I32), gsrc, sdst


def _moe(x2, xs, rloc, gate, cnt, w_gate_up, b_gate_up, w_down, b_down, tm_w, tm_e):
    n_win = x2.shape[0] // tm_w
    cnt2 = cnt.reshape(n_win, N_EXPERTS, LANES)[:, :, 0]
    tables = _moe_tables(cnt2, _window_cap(tm_w), x2.shape[0] * TOP_K, tm_e)
    ys = _moe_expert(tables, xs, w_gate_up, b_gate_up, w_down, b_down, tm_e)
    return _moe_combine(x2, ys, rloc, gate, tm_w)


def _pick(n, pref):
    return pref if n % pref == 0 else n


def _group(x, mk_bf, mv_bf, attn_fn, attn_tk, s0, l, p, bf, lam0):
    b, t, d = x.shape
    x2d = x.reshape(b * t, d)
    (qs, kb, vb, k32, v32, hq, hk, hv, lf, gt) = _mixer_in(
        x2d, p["norm_mix_g"][l], bf["w_in"][l], p["a_qnorm_g"][l], p["a_knorm_g"][l],
        p["hgrn_lb_logits"], l, _pick(b * t, 512), attn_tk)
    r3 = lambda a: a.reshape(b, t, -1)
    oa = attn_fn(r3(qs), r3(kb), vb if attn_tk else r3(vb))
    ob, s_new = _hgrn(r3(hq), r3(hk), r3(hv), r3(lf), s0, _pick(t, 128))
    tm_w = _pick(t, 256)
    tm_e = 512 if b * t >= 4096 else 64
    x2, xs, rloc, gate, cnt = _post(
        x2d, oa.reshape(b * t, -1), ob.reshape(b * t, -1), gt, p["a_subln_g"][l], p["b_gnorm_g"][l],
        bf["w_out"][l], p["norm_x_g"][l], bf["w_xq"][l], p["xq_norm_g"][l], mk_bf, mv_bf,
        bf["w_xo"][l], p["norm_ffn_g"][l], p["w_router"][l], p["b_router"][l], lam0,
        tm_w, t)
    y = _moe(x2, xs, rloc, gate, cnt, p["w_gate_up"][l], p["b_gate_up"][l], p["w_down"][l],
             p["b_down"][l], tm_w, tm_e)
    return (y.reshape(b, t, d), k32.reshape(b, t, 2 * A_HEADS, A_HEAD_DIM),
            v32.reshape(b, t, A_HEADS, 2 * A_HEAD_DIM), s_new)


def kernel(x_prompt, x_sample, mem_prompt, cache_attn_k, cache_attn_v, state_hgrn, cache_mem_k, cache_mem_v, norm_mix_g, w_in, a_qnorm_g, a_knorm_g, lambda_q1, lambda_k1, lambda_q2, lambda_k2, a_subln_g, hgrn_lb_logits, b_gnorm_g, w_out, norm_x_g, norm_mem_g, w_xq, w_mk, w_mv, xq_norm_g, xk_norm_g, w_xo, norm_ffn_g, w_router, b_router, w_gate_up, b_gate_up, w_down, b_down):
    p = dict(norm_mix_g=norm_mix_g, a_qnorm_g=a_qnorm_g, a_knorm_g=a_knorm_g, a_subln_g=a_subln_g,
             hgrn_lb_logits=hgrn_lb_logits, b_gnorm_g=b_gnorm_g, norm_x_g=norm_x_g,
             xq_norm_g=xq_norm_g, norm_ffn_g=norm_ffn_g, w_router=w_router, b_router=b_router,
             w_gate_up=w_gate_up, b_gate_up=b_gate_up, w_down=w_down, b_down=b_down)
    bf = dict(w_in=w_in.astype(BF16), w_out=w_out.astype(BF16), w_xq=w_xq.astype(BF16),
              w_xo=w_xo.astype(BF16))
    depth = w_in.shape[0]
    b, s, d = x_prompt.shape
    db, ds, _ = x_sample.shape
    n_mem = mem_prompt.shape[1]
    xp, xs = x_prompt, x_sample
    outs = [[] for _ in range(8)]
    for l in range(depth):
        lam0 = 0.8 - 0.6 * math.exp(-0.3 * l)
        lam_vec = jnp.stack([lambda_q1[l], lambda_k1[l], lambda_q2[l], lambda_k2[l]])

        mk, mv = _mem_kv(mem_prompt.reshape(b * n_mem, d), norm_mem_g[l], w_mk[l].astype(BF16),
                         w_mv[l].astype(BF16), xk_norm_g[l], _pick(b * n_mem, 512))
        tk = 2 * LANES
        attn_p = lambda q, k, vt: _attn_prompt(lam_vec, q, k, vt, lam0, 2 * tk, tk)
        s0 = jnp.zeros((b, B_HEADS, B_DIM, B_DIM), F32)
        xp, kp, vp, sp = _group(xp, mk.reshape(b, n_mem, d).astype(BF16),
                                mv.reshape(b, n_mem, d).astype(BF16), attn_p, tk, s0, l, p, bf, lam0)

        kc = cache_attn_k[l].reshape(db, -1, A_WIDTH).astype(BF16)
        vc = cache_attn_v[l].reshape(db, -1, A_WIDTH).astype(BF16)
        attn_s = lambda q, k, v: _attn_sample(lam_vec, q, kc, vc, k, v, lam0)
        xs, ks, vs, ss = _group(xs, cache_mem_k[l].reshape(db, n_mem, d).astype(BF16),
                                cache_mem_v[l].reshape(db, n_mem, d).astype(BF16), attn_s, None,
                                state_hgrn[l], l, p, bf, lam0)
        for lst, val in zip(outs, (kp, vp, sp, mk.reshape(b, n_mem, X_HEADS, -1),
                                   mv.reshape(b, n_mem, X_HEADS, -1), ks, vs, ss)):
            lst.append(val)
    return (xp, xs) + tuple(jnp.stack(o) for o in outs)
```

```python
import functools
import math

import jax
import jax.numpy as jnp
from jax import lax
from jax.experimental import pallas as pl
from jax.experimental.pallas import tpu as pltpu

F32 = jnp.float32
BF16 = jnp.bfloat16
I32 = jnp.int32

EPS = 1e-6
CHUNK = 64
A_HEADS = 4
A_HEAD_DIM = 64
A_WIDTH = A_HEADS * 2 * A_HEAD_DIM
B_HEADS = 4
B_DIM = 128
X_HEADS = 4
N_EXPERTS = 32
TOP_K = 4
SWIGLU_LIMIT = 7.0
SWIGLU_ALPHA = 1.702

LANES = 128
ROW_BLOCK = 8
VMEM_LIMIT = 56 * 1024 * 1024


def _cparams(sem):
    return pltpu.CompilerParams(dimension_semantics=sem, vmem_limit_bytes=VMEM_LIMIT)


def _const_spec(shape):
    zeros = (0,) * len(shape)
    return pl.BlockSpec(shape, lambda *_: zeros)


def _rms(x, g):
    return x * lax.rsqrt(jnp.mean(x * x, axis=-1, keepdims=True) + EPS) * g


def _dot(a, b):
    return jnp.dot(a, b, preferred_element_type=F32)


def _dot_nt(a, b):
    return lax.dot_general(a, b, (((1,), (1,)), ((), ())), preferred_element_type=F32)


def _dot_tn(a, b):
    return lax.dot_general(a, b, (((0,), (0,)), ((), ())), preferred_element_type=F32)


def _sigmoid(x):
    return 1.0 / (1.0 + jnp.exp(-x))


def _mixer_in_kernel(x_ref, g_ref, w_ref, gq_ref, gk_ref, lbl_ref, seg_ref,
                     qs_ref, kb_ref, vb_ref, k32_ref, v32_ref,
                     hq_ref, hk_ref, hv_ref, lf_ref, gt_ref, *, layer, tk):
    hb = _rms(x_ref[...], g_ref[...]).astype(BF16)

    def proj(j):
        return _dot(hb, w_ref[:, j * A_WIDTH:(j + 1) * A_WIDTH])

    seg = seg_ref[...]

    def head_norm(a, g):
        ss = _dot((a * a).astype(BF16), seg)
        return a * lax.rsqrt(ss * (1.0 / A_HEAD_DIM) + EPS) * g

    q = head_norm(proj(0), gq_ref[...])
    qs_ref[...] = (q * (A_HEAD_DIM ** -0.5 * math.log2(math.e))).astype(BF16)
    k = head_norm(proj(1), gk_ref[...])
    for hh in range(2 * A_HEADS):
        k32_ref[:, hh, :] = k[:, hh * A_HEAD_DIM:(hh + 1) * A_HEAD_DIM]
    kb_ref[...] = k.astype(BF16)
    v = proj(2)
    for hh in range(A_HEADS):
        v32_ref[:, hh, :] = v[:, hh * LANES:(hh + 1) * LANES]
    if tk is None:
        vb_ref[...] = v.astype(BF16)
    else:
        vt = v.T.astype(BF16)
        for c in range(vb_ref.shape[0]):
            vb_ref[c] = vt[:, c * tk:(c + 1) * tk]

    lbl = lbl_ref[...]
    e = jnp.exp(lbl - jnp.max(lbl, axis=0, keepdims=True))
    lb = jnp.sum(e[:layer + 1], axis=0, keepdims=True) / jnp.sum(e, axis=0, keepdims=True)

    qb = proj(3)
    hq_ref[...] = (qb * _sigmoid(qb)).astype(BF16)
    f = lb + (1.0 - lb) * _sigmoid(proj(4))
    hk_ref[...] = (1.0 - f).astype(BF16)
    lf_ref[...] = jnp.log(f)
    hv_ref[...] = proj(5).astype(BF16)
    gb = proj(6)
    gt_ref[...] = (gb * _sigmoid(gb)).astype(BF16)


def _mixer_in(x2d, norm_g, w_in_bf, gq, gk, lb_logits, layer, tm, tk):
    t, d = x2d.shape
    w = A_WIDTH
    seg = (jnp.arange(w)[:, None] // A_HEAD_DIM == jnp.arange(w)[None, :] // A_HEAD_DIM).astype(BF16)
    row = lambda i: (i, 0)
    out_dt = [BF16, BF16, BF16, F32, F32, BF16, BF16, BF16, F32, BF16]
    out_shape = [jax.ShapeDtypeStruct((t, w), dt) for dt in out_dt]
    out_specs = [pl.BlockSpec((tm, w), row) for _ in out_dt]
    row3 = lambda i: (i, 0, 0)
    out_shape[3] = jax.ShapeDtypeStruct((t, 2 * A_HEADS, A_HEAD_DIM), F32)
    out_specs[3] = pl.BlockSpec((tm, 2 * A_HEADS, A_HEAD_DIM), row3)
    out_shape[4] = jax.ShapeDtypeStruct((t, A_HEADS, 2 * A_HEAD_DIM), F32)
    out_specs[4] = pl.BlockSpec((tm, A_HEADS, 2 * A_HEAD_DIM), row3)
    if tk is not None:
        out_shape[2] = jax.ShapeDtypeStruct((t // tk, w, tk), BF16)
        out_specs[2] = pl.BlockSpec((tm // tk, w, tk), lambda i: (i, 0, 0))
    return pl.pallas_call(
        functools.partial(_mixer_in_kernel, layer=layer, tk=tk),
        out_shape=out_shape,
        grid=(t // tm,),
        in_specs=[pl.BlockSpec((tm, d), row), _const_spec((1, d)), _const_spec(w_in_bf.shape),
                  _const_spec((1, w)), _const_spec((1, w)),
                  _const_spec(lb_logits.shape), _const_spec((w, w))],
        out_specs=out_specs,
        compiler_params=_cparams(("parallel",)),
        name="mixer_in",
    )(x2d, norm_g.reshape(1, d), w_in_bf,
      jnp.tile(gq, 2 * A_HEADS).reshape(1, w), jnp.tile(gk, 2 * A_HEADS).reshape(1, w),
      lb_logits, seg)


def _lambda(lam_ref, lam0):
    lv = lam_ref[...]
    e1 = jnp.exp(jnp.sum(lv[0:1] * lv[1:2], axis=-1, keepdims=True))
    e2 = jnp.exp(jnp.sum(lv[2:3] * lv[3:4], axis=-1, keepdims=True))
    return e1 - e2 + lam0


def _split_maps(k):
    first = lax.broadcasted_iota(I32, k.shape, 1) < A_HEAD_DIM
    zero = jnp.zeros_like(k)
    return jnp.where(first, k, zero), jnp.where(first, zero, k)


def _attn_prompt_kernel(lam_ref, bias_ref, q_ref, k_ref, vt_ref, o_ref,
                        sa_ref, sb_ref, pa_ref, pb_ref, acc_ref, *, tq, tk, lam0):
    assert tq == 2 * tk
    qi = pl.program_id(2)
    q = q_ref[0]
    s_bufs = (sa_ref, sb_ref)
    p_bufs = (pa_ref, pb_ref)

    def phase(j, par, mx_cur, ml, masked, prefetch):
        s_ref, p_ref = s_bufs[par], p_bufs[par]
        sn_ref, pp_ref = s_bufs[1 - par], p_bufs[1 - par]
        vt = vt_ref[jnp.maximum(j - 1, 0)]
        mx_next = []
        if prefetch:
            k = k_ref[0, pl.ds(pl.multiple_of((j + 1) * tk, tk), tk), :]
            for mp, km in enumerate(_split_maps(k)):
                s = _dot_nt(km, q)
                sn_ref[mp] = s
                mx_next.append(jnp.max(s, axis=0, keepdims=True))
        new_ml = []
        for mp in range(2):
            m, l = ml[mp]
            pv = _dot(vt, pp_ref[mp])
            s = s_ref[mp]
            s_max = mx_cur[mp]
            if masked is not None:
                s = s + bias_ref[masked]
                s_max = jnp.max(s, axis=0, keepdims=True)
            m_new = jnp.maximum(m, s_max)
            alpha = jnp.exp2(m - m_new)
            p = jnp.exp2(s - m_new)
            p_ref[mp] = p.astype(BF16)
            acc_ref[mp] = alpha * (acc_ref[mp] + pv)
            new_ml.append((m_new, alpha * l + jnp.sum(p, axis=0, keepdims=True)))
        return tuple(mx_next), tuple(new_ml)

    k0 = k_ref[0, pl.ds(0, tk), :]
    mx = []
    for mp, km in enumerate(_split_maps(k0)):
        s = _dot_nt(km, q)
        sa_ref[mp] = s
        mx.append(jnp.max(s, axis=0, keepdims=True))
    pb_ref[...] = jnp.zeros_like(pb_ref)
    acc_ref[...] = jnp.zeros_like(acc_ref)
    ml = tuple((jnp.full((1, tq), -jnp.inf, F32), jnp.zeros((1, tq), F32)) for _ in range(2))

    def pair(t, carry):
        mx_a, ml = carry
        mx_b, ml = phase(2 * t, 0, mx_a, ml, None, True)
        mx_a, ml = phase(2 * t + 1, 1, mx_b, ml, None, True)
        return mx_a, ml

    mx_a, ml = lax.fori_loop(0, qi, pair, (tuple(mx), ml))
    mx_b, ml = phase(2 * qi, 0, mx_a, ml, 0, True)
    _, ml = phase(2 * qi + 1, 1, mx_b, ml, 1, False)
    vt = vt_ref[2 * qi + 1]
    (_, l1), (_, l2) = ml
    o1 = (acc_ref[0] + _dot(vt, pb_ref[0])) / l1
    o2 = (acc_ref[1] + _dot(vt, pb_ref[1])) / l2
    o_ref[0] = (o1 - _lambda(lam_ref, lam0) * o2).T.astype(o_ref.dtype)


def _attn_prompt(lam_vec, qs, kb, vt, lam0, tq, tk):
    b, s, w = qs.shape
    nkb = s // tk
    blk = lambda bi, hi, qi: (bi, qi, hi)
    kc = (jnp.arange(2 * tk) // CHUNK).reshape(2, tk, 1)
    qc = (jnp.arange(tq) // CHUNK).reshape(1, 1, tq)
    bias = jnp.where(kc <= qc, 0.0, -jnp.inf).astype(F32)
    return pl.pallas_call(
        functools.partial(_attn_prompt_kernel, tq=tq, tk=tk, lam0=lam0),
        out_shape=jax.ShapeDtypeStruct((b, s, w), F32),
        grid=(b, A_HEADS, s // tq),
        in_specs=[_const_spec(lam_vec.shape), _const_spec(bias.shape),
                  pl.BlockSpec((1, tq, LANES), blk),
                  pl.BlockSpec((1, s, LANES), lambda bi, hi, qi: (bi, 0, hi)),
                  pl.BlockSpec((nkb, LANES, tk), lambda bi, hi, qi: (bi, hi, 0))],
        out_specs=pl.BlockSpec((1, tq, LANES), blk),
        scratch_shapes=[pltpu.VMEM((2, tk, tq), F32), pltpu.VMEM((2, tk, tq), F32),
                        pltpu.VMEM((2, tk, tq), BF16), pltpu.VMEM((2, tk, tq), BF16),
                        pltpu.VMEM((2, LANES, tq), F32)],
        compiler_params=_cparams(("parallel", "parallel", "arbitrary")),
        name="attn_prompt",
    )(lam_vec, bias, qs, kb, vt)


def _attn_sample_kernel(lam_ref, q_ref, kc_ref, vc_ref, kn_ref, vn_ref, o_ref, *, lam0):
    q = q_ref[0]
    vc = vc_ref[0]
    vn = vn_ref[0]
    outs = []
    for kcm, knm in zip(_split_maps(kc_ref[0]), _split_maps(kn_ref[0])):
        sc = _dot_nt(q, kcm)
        sn = _dot_nt(q, knm)
        m = jnp.maximum(jnp.max(sc, axis=-1, keepdims=True), jnp.max(sn, axis=-1, keepdims=True))
        pc = jnp.exp2(sc - m)
        pn = jnp.exp2(sn - m)
        l = jnp.sum(pc, axis=-1, keepdims=True) + jnp.sum(pn, axis=-1, keepdims=True)
        outs.append((_dot(pc.astype(BF16), vc) + _dot(pn.astype(BF16), vn)) / l)
    o_ref[0] = (outs[0] - _lambda(lam_ref, lam0) * outs[1]).astype(o_ref.dtype)


def _attn_sample(lam_vec, qs, kc, vc, kn, vn, lam0):
    b, t, w = qs.shape
    past = kc.shape[1]
    blk = lambda bi, hi: (bi, 0, hi)
    return pl.pallas_call(
        functools.partial(_attn_sample_kernel, lam0=lam0),
        out_shape=jax.ShapeDtypeStruct((b, t, w), F32),
        grid=(b, A_HEADS),
        in_specs=[_const_spec(lam_vec.shape), pl.BlockSpec((1, t, LANES), blk),
                  pl.BlockSpec((1, past, LANES), blk), pl.BlockSpec((1, past, LANES), blk),
                  pl.BlockSpec((1, t, LANES), blk), pl.BlockSpec((1, t, LANES), blk)],
        out_specs=pl.BlockSpec((1, t, LANES), blk),
        compiler_params=_cparams(("parallel", "parallel")),
        name="attn_sample",
    )(lam_vec, qs, kc, vc, kn, vn)


def _cumsum_rows(x):
    c = x.shape[0]
    row = lax.broadcasted_iota(I32, x.shape, 0)
    d = 1
    while d < c:
        x = x + jnp.where(row >= d, pltpu.roll(x, d, axis=0), 0.0)
        d *= 2
    return x


def _block_ref_rows(cum, m):
    c, n = cum.shape
    if m >= 8:
        r = cum.reshape(c // (2 * m), 2 * m, n)[:, m - 1:m, :]
        return jnp.broadcast_to(r, (c // (2 * m), 2 * m, n)).reshape(c, n)
    pos = lax.broadcasted_iota(I32, cum.shape, 0) % (2 * m)
    out = cum
    for p in range(2 * m):
        shift = p - (m - 1)
        if shift != 0:
            out = jnp.where(pos == p, pltpu.roll(cum, shift % c, axis=0), out)
    return out


def _hgrn_kernel(hq_ref, hk_ref, hv_ref, lf_ref, s0_ref, o_ref, st_ref, stt_ref, *, c):
    ci = pl.program_id(1)

    @pl.when(ci == 0)
    def _():
        for h in range(B_HEADS):
            stt_ref[h] = s0_ref[0, h].T

    row = lax.broadcasted_iota(I32, (c, B_DIM), 0)
    r2 = lax.broadcasted_iota(I32, (c, c), 0)
    c2 = lax.broadcasted_iota(I32, (c, c), 1)
    levels = []
    m = c // 2
    while m >= 1:
        levels.append((m, (row % (2 * m)) >= m, (r2 // (2 * m)) == (c2 // (2 * m))))
        m //= 2
    for h in range(B_HEADS):
        sl = slice(h * B_DIM, (h + 1) * B_DIM)
        hq = hq_ref[0, :, sl].astype(F32)
        hk = hk_ref[0, :, sl].astype(F32)
        hv = hv_ref[0, :, sl]
        cum = _cumsum_rows(lf_ref[0, :, sl])
        last = cum[c - 1:c, :]

        a = None
        for m, upper, same_block in levels:
            d = cum - _block_ref_rows(cum, m)
            qd = hq * jnp.exp(jnp.where(upper, d, -jnp.inf))
            kd = hk * jnp.exp(jnp.where(upper, -jnp.inf, -d))
            p = _dot_nt(qd.astype(BF16), kd.astype(BF16))
            a = jnp.where(same_block, p, 0.0 if a is None else a)

        stt = stt_ref[h]
        o = _dot_nt((hq * jnp.exp(cum)).astype(BF16), stt.astype(BF16))
        o = o + _dot(a.astype(BF16), hv)
        o = o + jnp.sum(hq * hk, axis=-1, keepdims=True) * hv.astype(F32)
        o_ref[0, :, sl] = o.astype(o_ref.dtype)

        kl = (hk * jnp.exp(last - cum)).astype(BF16)
        stt_ref[h] = stt * jnp.exp(last) + _dot_tn(hv, kl)

    @pl.when(ci == pl.num_programs(1) - 1)
    def _():
        for h in range(B_HEADS):
            st_ref[0, h] = stt_ref[h].T


def _hgrn(hq, hk, hv, lf, s0, c):
    b, s, w = hq.shape
    blk = lambda bi, ci: (bi, ci, 0)
    sblk = lambda bi, ci: (bi, 0, 0, 0)
    return pl.pallas_call(
        functools.partial(_hgrn_kernel, c=c),
        out_shape=[jax.ShapeDtypeStruct((b, s, w), F32),
                   jax.ShapeDtypeStruct((b, B_HEADS, B_DIM, B_DIM), F32)],
        grid=(b, s // c),
        in_specs=[pl.BlockSpec((1, c, w), blk)] * 4 + [pl.BlockSpec((1, B_HEADS, B_DIM, B_DIM), sblk)],
        out_specs=[pl.BlockSpec((1, c, w), blk), pl.BlockSpec((1, B_HEADS, B_DIM, B_DIM), sblk)],
        scratch_shapes=[pltpu.VMEM((B_HEADS, B_DIM, B_DIM), F32)],
        compiler_params=_cparams(("parallel", "arbitrary")),
        name="hgrn",
    )(hq, hk, hv, lf, s0)


def _mem_kv_kernel(m_ref, g_ref, wk_ref, wv_ref, gk_ref, k_ref, v_ref):
    hm = _rms(m_ref[...], g_ref[...]).astype(BF16)
    k = _dot(hm, wk_ref[...])
    gk = gk_ref[...]
    hd = k.shape[-1] // X_HEADS
    for h in range(X_HEADS):
        sl = slice(h * hd, (h + 1) * hd)
        k_ref[:, sl] = _rms(k[:, sl], gk)
    v_ref[...] = _dot(hm, wv_ref[...])


def _mem_kv(mem2d, norm_g, w_mk_bf, w_mv_bf, xk_g, tm):
    t, d = mem2d.shape
    row = lambda i: (i, 0)
    return pl.pallas_call(
        _mem_kv_kernel,
        out_shape=[jax.ShapeDtypeStruct((t, d), F32)] * 2,
        grid=(t // tm,),
        in_specs=[pl.BlockSpec((tm, d), row), _const_spec((1, d)), _const_spec(w_mk_bf.shape),
                  _const_spec(w_mv_bf.shape), _const_spec((1, xk_g.shape[-1]))],
        out_specs=[pl.BlockSpec((tm, d), row)] * 2,
        compiler_params=_cparams(("parallel",)),
        name="mem_kv",
    )(mem2d, norm_g.reshape(1, d), w_mk_bf, w_mv_bf, xk_g.reshape(1, -1))


def _post_kernel(x_ref, oa_ref, ob_ref, gt_ref, sg_ref, gg_ref, wo_ref,
                 nx_ref, wq_ref, xq_ref, mk_ref, mv_ref, wxo_ref,
                 nf_ref, wr_ref, br_ref,
                 x2_ref, xs_ref, rloc_ref, gate_ref, cnt_ref, *, lam0):
    ya = []
    yb = []
    for h in range(A_HEADS):
        sl = slice(h * LANES, (h + 1) * LANES)
        ya.append((_rms(oa_ref[:, sl], sg_ref[...]) * (1.0 - lam0)).astype(BF16))
        yb.append((_rms(ob_ref[:, sl], gg_ref[...]) * gt_ref[:, sl].astype(F32)).astype(BF16))
    y = jnp.concatenate(ya + yb, axis=-1)
    x1 = x_ref[...] + _dot(y, wo_ref[...])

    q = _dot(_rms(x1, nx_ref[...]).astype(BF16), wq_ref[...])
    hd = q.shape[-1] // X_HEADS
    os_ = []
    for h in range(X_HEADS):
        sl = slice(h * hd, (h + 1) * hd)
        qh = _rms(q[:, sl], xq_ref[...]).astype(BF16)
        s = _dot_nt(qh, mk_ref[0, :, sl]) * (hd ** -0.5)
        p = jnp.exp(s - jnp.max(s, axis=-1, keepdims=True))
        p = p / jnp.sum(p, axis=-1, keepdims=True)
        os_.append(_dot(p.astype(BF16), mv_ref[0, :, sl]).astype(BF16))
    x2 = x1 + _dot(jnp.concatenate(os_, axis=-1), wxo_ref[...])
    x2_ref[...] = x2

    h2 = _rms(x2, nf_ref[...])
    w = h2.shape[0]
    h_hi = h2.astype(BF16)
    h_lo = (h2 - h_hi.astype(F32)).astype(BF16)
    wr = wr_ref[...]
    w_hi = wr.astype(BF16)
    w_lo = (wr - w_hi.astype(F32)).astype(BF16)
    a = _dot_nt(jnp.concatenate([w_hi, w_lo], axis=0), h_hi)
    logits = a[:N_EXPERTS] + a[N_EXPERTS:] + _dot_nt(w_hi, h_lo) + br_ref[...]
    erow = lax.broadcasted_iota(I32, logits.shape, 0)
    picks, vals = [], []
    for k in range(TOP_K):
        mx = jnp.max(logits, axis=0, keepdims=True)
        ix = jnp.min(jnp.where(logits == mx, erow, N_EXPERTS), axis=0, keepdims=True)
        pick = erow == ix
        picks.append(pick)
        vals.append(mx)
        logits = jnp.where(pick, -jnp.inf, logits)
    ex = [jnp.exp(v - vals[0]) for v in vals]
    den = ex[0] + ex[1] + ex[2] + ex[3]

    onehot = jnp.where(picks[0] | picks[1] | picks[2] | picks[3], 1.0, 0.0)
    earlier = (lax.broadcasted_iota(I32, (w, w), 0) < lax.broadcasted_iota(I32, (w, w), 1))
    rank = _dot(onehot.astype(BF16), jnp.where(earlier, 1.0, 0.0).astype(BF16))
    cnt = jnp.sum(onehot, axis=1, keepdims=True).astype(I32)
    pcnt = ((cnt + (ROW_BLOCK - 1)) // ROW_BLOCK) * ROW_BLOCK
    pc_b = jnp.broadcast_to(pcnt.astype(F32), onehot.shape)
    seg_off = _cumsum_rows(pc_b) - pc_b
    cap = xs_ref.shape[0]
    riota = lax.broadcasted_iota(I32, (cap, w), 0)
    sel = None
    lane8 = lax.broadcasted_iota(I32, rloc_ref.shape, 0)
    rloc_out = jnp.zeros(rloc_ref.shape, I32)
    gate_out = jnp.zeros(gate_ref.shape, F32)
    for k in range(TOP_K):
        rloc = jnp.sum(jnp.where(picks[k], seg_off + rank, 0.0), axis=0, keepdims=True).astype(I32)
        hit = riota == rloc
        sel = hit if sel is None else (sel | hit)
        rloc_out = jnp.where(lane8 == k, rloc, rloc_out)
        gate_out = jnp.where(lane8 == k, ex[k] / den, gate_out)
    xs_ref[...] = _dot(jnp.where(sel, 1.0, 0.0).astype(BF16), h_hi)
    rloc_ref[...] = rloc_out
    gate_ref[...] = gate_out
    cnt_ref[...] = jnp.broadcast_to(cnt, cnt_ref.shape)


def _window_cap(w):
    cap = w * TOP_K + N_EXPERTS * (ROW_BLOCK - 1) + 2 * ROW_BLOCK
    return -(-cap // LANES) * LANES


def _post(x2d, oa, ob, gt, subln_g, gn_g, w_out_bf, norm_x_g, w_xq_bf, xq_g, mk_bf, mv_bf,
          w_xo_bf, norm_ffn_g, w_router, b_router, lam0, tm, rows_per_batch):
    t, d = x2d.shape
    n_win = t // tm
    cap = _window_cap(tm)
    row = lambda i: (i, 0)
    tiles_per_batch = rows_per_batch // tm
    mem = lambda i: (i // tiles_per_batch, 0, 0)
    n_mem = mk_bf.shape[1]
    vec = lambda a: a.reshape(1, -1)
    return pl.pallas_call(
        functools.partial(_post_kernel, lam0=lam0),
        out_shape=[jax.ShapeDtypeStruct((t, d), F32),
                   jax.ShapeDtypeStruct((n_win * cap, d), F32),
                   jax.ShapeDtypeStruct((n_win * 8, tm), I32),
                   jax.ShapeDtypeStruct((n_win * 8, tm), F32),
                   jax.ShapeDtypeStruct((n_win * N_EXPERTS, LANES), I32)],
        grid=(n_win,),
        in_specs=[pl.BlockSpec((tm, d), row), pl.BlockSpec((tm, A_WIDTH), row),
                  pl.BlockSpec((tm, A_WIDTH), row), pl.BlockSpec((tm, A_WIDTH), row),
                  _const_spec((1, LANES)), _const_spec((1, LANES)), _const_spec(w_out_bf.shape),
                  _const_spec((1, d)), _const_spec(w_xq_bf.shape), _const_spec((1, xq_g.shape[-1])),
                  pl.BlockSpec((1, n_mem, d), mem), pl.BlockSpec((1, n_mem, d), mem),
                  _const_spec(w_xo_bf.shape),
                  _const_spec((1, d)), _const_spec((N_EXPERTS, d)), _const_spec((N_EXPERTS, 1))],
        out_specs=[pl.BlockSpec((tm, d), row), pl.BlockSpec((cap, d), row),
                   pl.BlockSpec((8, tm), row), pl.BlockSpec((8, tm), row),
                   pl.BlockSpec((N_EXPERTS, LANES), row)],
        compiler_params=_cparams(("parallel",)),
        name="post",
    )(x2d, oa, ob, gt, vec(subln_g), vec(gn_g), w_out_bf, vec(norm_x_g), w_xq_bf, vec(xq_g),
      mk_bf, mv_bf, w_xo_bf, vec(norm_ffn_g), w_router.T, b_router.reshape(N_EXPERTS, 1))


def _moe_expert_kernel(te_ref, tf_ref, tv_ref, g0_ref, gn_ref, sd_ref, xs_hbm,
                       wgu_ref, bgu_ref, wd_ref, bd_ref, ys_hbm,
                       xbuf, ybuf, wgu_bf, wd_bf, gsem, ssem, *, tm):
    i = pl.program_id(0)
    last = pl.num_programs(0) - 1
    slot = i % 2
    nb = tm // ROW_BLOCK
    valid = tv_ref[i] == 1
    next_valid = jnp.logical_and(i < last, tv_ref[jnp.minimum(i + 1, last)] == 1)

    def gather(tab_ref, s):
        for j in range(nb):
            r = pl.multiple_of(tab_ref[0, 0, j], ROW_BLOCK)
            pltpu.make_async_copy(xs_hbm.at[pl.ds(r, ROW_BLOCK)],
                                  xbuf.at[s, pl.ds(j * ROW_BLOCK, ROW_BLOCK)], gsem.at[s]).start()

    def scatter(s):
        for j in range(nb):
            r = pl.multiple_of(sd_ref[0, 0, j], ROW_BLOCK)
            pltpu.make_async_copy(ybuf.at[s, pl.ds(j * ROW_BLOCK, ROW_BLOCK)],
                                  ys_hbm.at[pl.ds(r, ROW_BLOCK)], ssem.at[s]).start()

    def wait_gather(s):
        pltpu.make_async_copy(xs_hbm.at[pl.ds(0, tm)], xbuf.at[s], gsem.at[s]).wait()

    def wait_scatter(s):
        pltpu.make_async_copy(ybuf.at[s], ys_hbm.at[pl.ds(0, tm)], ssem.at[s]).wait()

    @pl.when(jnp.logical_and(i == 0, valid))
    def _():
        gather(g0_ref, 0)

    @pl.when(jnp.logical_and(valid, tf_ref[i] == 1))
    def _():
        wgu_bf[...] = wgu_ref[0].astype(BF16)
        wd_bf[...] = wd_ref[0].astype(BF16)

    @pl.when(valid)
    def _():
        wait_gather(slot)

        @pl.when(next_valid)
        def _():
            gather(gn_ref, 1 - slot)

        de = wd_bf.shape[0]
        gu = _dot(xbuf[slot].astype(BF16), wgu_bf[...]) + bgu_ref[0]
        gate = jnp.minimum(gu[:, :de], SWIGLU_LIMIT)
        up = jnp.clip(gu[:, de:], -SWIGLU_LIMIT, SWIGLU_LIMIT)
        act = (up + 1.0) * gate * _sigmoid(SWIGLU_ALPHA * gate)
        ybuf[slot] = _dot(act.astype(BF16), wd_bf[...]) + bd_ref[0]

        @pl.when(i > 0)
        def _():
            wait_scatter(1 - slot)

        scatter(slot)

        @pl.when(jnp.logical_not(next_valid))
        def _():
            wait_scatter(slot)


def _moe_expert(tables, xs, w_gate_up, b_gate_up, w_down, b_down, tm):
    tile_expert, tile_first, tile_valid, gsrc, sdst = tables
    nt = tile_expert.shape[0]
    nb = tm // ROW_BLOCK
    d = xs.shape[1]
    ne, _, dgu = w_gate_up.shape
    de = w_down.shape[1]
    smem = lambda f: pl.BlockSpec((1, 1, nb), f, memory_space=pltpu.SMEM)
    wmap = lambda i, te, tf, tv: (te[i], 0, 0)
    grid_spec = pltpu.PrefetchScalarGridSpec(
        num_scalar_prefetch=3,
        grid=(nt,),
        in_specs=[smem(lambda i, te, tf, tv: (0, 0, 0)),
                  smem(lambda i, te, tf, tv: (jnp.minimum(i + 1, nt - 1), 0, 0)),
                  smem(lambda i, te, tf, tv: (i, 0, 0)),
                  pl.BlockSpec(memory_space=pl.ANY),
                  pl.BlockSpec((1, d, dgu), wmap), pl.BlockSpec((1, 1, dgu), wmap),
                  pl.BlockSpec((1, de, d), wmap), pl.BlockSpec((1, 1, d), wmap)],
        out_specs=pl.BlockSpec(memory_space=pl.ANY),
        scratch_shapes=[pltpu.VMEM((2, tm, d), F32), pltpu.VMEM((2, tm, d), F32),
                        pltpu.VMEM((d, dgu), BF16), pltpu.VMEM((de, d), BF16),
                        pltpu.SemaphoreType.DMA((2,)), pltpu.SemaphoreType.DMA((2,))],
    )
    return pl.pallas_call(
        functools.partial(_moe_expert_kernel, tm=tm),
        out_shape=jax.ShapeDtypeStruct(xs.shape, F32),
        grid_spec=grid_spec,
        input_output_aliases={6: 0},
        compiler_params=_cparams(("arbitrary",)),
        name="moe_expert",
    )(tile_expert, tile_first, tile_valid, gsrc.reshape(nt, 1, nb), gsrc.reshape(nt, 1, nb),
      sdst.reshape(nt, 1, nb), xs, w_gate_up, b_gate_up.reshape(ne, 1, dgu), w_down,
      b_down.reshape(ne, 1, d))


def _moe_combine_kernel(x_ref, ys_ref, rloc_ref, gate_ref, o_ref):
    cap = ys_ref.shape[0]
    w = x_ref.shape[0]
    riota = lax.broadcasted_iota(I32, (cap, w), 0)
    gt = jnp.zeros((cap, w), F32)
    for k in range(TOP_K):
        gt = jnp.where(riota == rloc_ref[k:k + 1, :], gate_ref[k:k + 1, :], gt)
    o_ref[...] = x_ref[...] + _dot_tn(gt.astype(BF16), ys_ref[...].astype(BF16))


def _moe_combine(x2, ys, rloc, gate, tm):
    t, d = x2.shape
    cap = _window_cap(tm)
    row = lambda i: (i, 0)
    return pl.pallas_call(
        _moe_combine_kernel,
        out_shape=jax.ShapeDtypeStruct((t, d), F32),
        grid=(t // tm,),
        in_specs=[pl.BlockSpec((tm, d), row), pl.BlockSpec((cap, d), row),
                  pl.BlockSpec((8, tm), row), pl.BlockSpec((8, tm), row)],
        out_specs=pl.BlockSpec((tm, d), row),
        compiler_params=_cparams(("parallel",)),
        name="moe_combine",
    )(x2, ys, rloc, gate)


def _moe_tables(cnt, cap, n_picks, tm_e):
    n_win = cnt.shape[0]
    bpt = tm_e // ROW_BLOCK
    nt = (n_picks // ROW_BLOCK + n_win * N_EXPERTS) // bpt + N_EXPERTS
    blk = (cnt + ROW_BLOCK - 1) // ROW_BLOCK
    seg_blk = jnp.cumsum(blk, axis=1) - blk
    cum_incl = jnp.cumsum(blk, axis=0)
    cum_excl = cum_incl - blk
    tot = cum_incl[-1]
    tiles = (tot + bpt - 1) // bpt
    tile_end = jnp.cumsum(tiles)
    tile_beg = tile_end - tiles
    ti = jnp.arange(nt, dtype=I32)
    te = jnp.minimum(jnp.sum((ti[:, None] >= tile_end[None, :]).astype(I32), axis=1), N_EXPERTS - 1)
    tv = ti < tile_end[-1]
    tf = jnp.concatenate([jnp.ones((1,), bool), te[1:] != te[:-1]])
    v = (ti - tile_beg[te])[:, None] * bpt + jnp.arange(bpt, dtype=I32)[None, :]
    ok = tv[:, None] & (v < tot[te][:, None])
    widx = jnp.sum((cum_incl.T[te][:, None, :] <= v[:, :, None]).astype(I32), axis=-1)
    onehot = jnp.minimum(widx, n_win - 1)[:, :, None] == jnp.arange(n_win, dtype=I32)
    base = (jnp.arange(n_win, dtype=I32) * cap)[None, :] + (seg_blk.T[te] - cum_excl.T[te]) * ROW_BLOCK
    row = jnp.sum(jnp.where(onehot, base[:, None, :], 0), axis=-1) + v * ROW_BLOCK
    assert n_win >= bpt
    spare = jnp.arange(bpt, dtype=I32)[None, :] * cap + (cap - 2 * ROW_BLOCK)
    gsrc = jnp.where(ok, row, cap - ROW_BLOCK).astype(I32)
    sdst = jnp.where(ok, row, spare).astype(I32)
    return te.astype(I32), tf.astype(I32), tv.astype(I32), gsrc, sdst


def _moe(x2, xs, rloc, gate, cnt, w_gate_up, b_gate_up, w_down, b_down, tm_w, tm_e):
    n_win = x2.shape[0] // tm_w
    cnt2 = cnt.reshape(n_win, N_EXPERTS, LANES)[:, :, 0]
    tables = _moe_tables(cnt2, _window_cap(tm_w), x2.shape[0] * TOP_K, tm_e)
    ys = _moe_expert(tables, xs, w_gate_up, b_gate_up, w_down, b_down, tm_e)
    return _moe_combine(x2, ys, rloc, gate, tm_w)


def _pick(n, pref):
    return pref if n % pref == 0 else n


def _group(x, mk_bf, mv_bf, attn_fn, attn_tk, s0, l, p, bf, lam0):
    b, t, d = x.shape
    x2d = x.reshape(b * t, d)
    (qs, kb, vb, k32, v32, hq, hk, hv, lf, gt) = _mixer_in(
        x2d, p["norm_mix_g"][l], bf["w_in"][l], p["a_qnorm_g"][l], p["a_knorm_g"][l],
        p["hgrn_lb_logits"], l, _pick(b * t, 512), attn_tk)
    r3 = lambda a: a.reshape(b, t, -1)
    oa = attn_fn(r3(qs), r3(kb), vb if attn_tk else r3(vb))
    ob, s_new = _hgrn(r3(hq), r3(hk), r3(hv), r3(lf), s0, _pick(t, 128))
    tm_w = _pick(t, 512)
    tm_e = 512 if b * t >= 4096 else 64
    x2, xs, rloc, gate, cnt = _post(
        x2d, oa.reshape(b * t, -1), ob.reshape(b * t, -1), gt, p["a_subln_g"][l], p["b_gnorm_g"][l],
        bf["w_out"][l], p["norm_x_g"][l], bf["w_xq"][l], p["xq_norm_g"][l], mk_bf, mv_bf,
        bf["w_xo"][l], p["norm_ffn_g"][l], p["w_router"][l], p["b_router"][l], lam0,
        tm_w, t)
    y = _moe(x2, xs, rloc, gate, cnt, p["w_gate_up"][l], p["b_gate_up"][l], p["w_down"][l],
             p["b_down"][l], tm_w, tm_e)
    return (y.reshape(b, t, d), k32.reshape(b, t, 2 * A_HEADS, A_HEAD_DIM),
            v32.reshape(b, t, A_HEADS, 2 * A_HEAD_DIM), s_new)


def kernel(x_prompt, x_sample, mem_prompt, cache_attn_k, cache_attn_v, state_hgrn, cache_mem_k, cache_mem_v, norm_mix_g, w_in, a_qnorm_g, a_knorm_g, lambda_q1, lambda_k1, lambda_q2, lambda_k2, a_subln_g, hgrn_lb_logits, b_gnorm_g, w_out, norm_x_g, norm_mem_g, w_xq, w_mk, w_mv, xq_norm_g, xk_norm_g, w_xo, norm_ffn_g, w_router, b_router, w_gate_up, b_gate_up, w_down, b_down):
    p = dict(norm_mix_g=norm_mix_g, a_qnorm_g=a_qnorm_g, a_knorm_g=a_knorm_g, a_subln_g=a_subln_g,
             hgrn_lb_logits=hgrn_lb_logits, b_gnorm_g=b_gnorm_g, norm_x_g=norm_x_g,
             xq_norm_g=xq_norm_g, norm_ffn_g=norm_ffn_g, w_router=w_router, b_router=b_router,
             w_gate_up=w_gate_up, b_gate_up=b_gate_up, w_down=w_down, b_down=b_down)
    bf = dict(w_in=w_in.astype(BF16), w_out=w_out.astype(BF16), w_xq=w_xq.astype(BF16),
              w_xo=w_xo.astype(BF16))
    depth = w_in.shape[0]
    b, s, d = x_prompt.shape
    db, ds, _ = x_sample.shape
    n_mem = mem_prompt.shape[1]
    xp, xs = x_prompt, x_sample
    outs = [[] for _ in range(8)]
    for l in range(depth):
        lam0 = 0.8 - 0.6 * math.exp(-0.3 * l)
        lam_vec = jnp.stack([lambda_q1[l], lambda_k1[l], lambda_q2[l], lambda_k2[l]])

        mk, mv = _mem_kv(mem_prompt.reshape(b * n_mem, d), norm_mem_g[l], w_mk[l].astype(BF16),
                         w_mv[l].astype(BF16), xk_norm_g[l], _pick(b * n_mem, 512))
        tk = 4 * LANES
        attn_p = lambda q, k, vt: _attn_prompt(lam_vec, q, k, vt, lam0, 2 * tk, tk)
        s0 = jnp.zeros((b, B_HEADS, B_DIM, B_DIM), F32)
        xp, kp, vp, sp = _group(xp, mk.reshape(b, n_mem, d).astype(BF16),
                                mv.reshape(b, n_mem, d).astype(BF16), attn_p, tk, s0, l, p, bf, lam0)

        kc = cache_attn_k[l].reshape(db, -1, A_WIDTH).astype(BF16)
        vc = cache_attn_v[l].reshape(db, -1, A_WIDTH).astype(BF16)
        attn_s = lambda q, k, v: _attn_sample(lam_vec, q, kc, vc, k, v, lam0)
        xs, ks, vs, ss = _group(xs, cache_mem_k[l].reshape(db, n_mem, d).astype(BF16),
                                cache_mem_v[l].reshape(db, n_mem, d).astype(BF16), attn_s, None,
                                state_hgrn[l], l, p, bf, lam0)
        for lst, val in zip(outs, (kp, vp, sp, mk.reshape(b, n_mem, X_HEADS, -1),
                                   mv.reshape(b, n_mem, X_HEADS, -1), ks, vs, ss)):
            lst.append(val)
    return (xp, xs) + tuple(jnp.stack(o) for o in outs)
```

```python
import functools
import math

import jax
import jax.numpy as jnp
from jax import lax
from jax.experimental import pallas as pl
from jax.experimental.pallas import tpu as pltpu

F32 = jnp.float32
BF16 = jnp.bfloat16
I32 = jnp.int32

EPS = 1e-6
CHUNK = 64
A_HEADS = 4
A_HEAD_DIM = 64
A_WIDTH = A_HEADS * 2 * A_HEAD_DIM
B_HEADS = 4
B_DIM = 128
X_HEADS = 4
N_EXPERTS = 32
TOP_K = 4
SWIGLU_LIMIT = 7.0
SWIGLU_ALPHA = 1.702

LANES = 128
ROW_BLOCK = 8
VMEM_LIMIT = 56 * 1024 * 1024


def _cparams(sem):
    return pltpu.CompilerParams(dimension_semantics=sem, vmem_limit_bytes=VMEM_LIMIT)


def _const_spec(shape):
    zeros = (0,) * len(shape)
    return pl.BlockSpec(shape, lambda *_: zeros)


def _rms(x, g):
    return x * lax.rsqrt(jnp.mean(x * x, axis=-1, keepdims=True) + EPS) * g


def _dot(a, b):
    return jnp.dot(a, b, preferred_element_type=F32)


def _dot_nt(a, b):
    return lax.dot_general(a, b, (((1,), (1,)), ((), ())), preferred_element_type=F32)


def _dot_tn(a, b):
    return lax.dot_general(a, b, (((0,), (0,)), ((), ())), preferred_element_type=F32)


def _sigmoid(x):
    return 1.0 / (1.0 + jnp.exp(-x))


def _mixer_in_kernel(x_ref, g_ref, w_ref, gq_ref, gk_ref, lbl_ref, seg_ref,
                     qs_ref, kb_ref, vb_ref, k32_ref, v32_ref,
                     hq_ref, hk_ref, hv_ref, lf_ref, gt_ref, *, layer, tk):
    hb = _rms(x_ref[...], g_ref[...]).astype(BF16)

    def proj(j):
        return _dot(hb, w_ref[:, j * A_WIDTH:(j + 1) * A_WIDTH])

    seg = seg_ref[...]

    def head_norm(a, g):
        ss = _dot((a * a).astype(BF16), seg)
        return a * lax.rsqrt(ss * (1.0 / A_HEAD_DIM) + EPS) * g

    q = head_norm(proj(0), gq_ref[...])
    qs_ref[...] = (q * (A_HEAD_DIM ** -0.5 * math.log2(math.e))).astype(BF16)
    k = head_norm(proj(1), gk_ref[...])
    for hh in range(2 * A_HEADS):
        k32_ref[:, hh, :] = k[:, hh * A_HEAD_DIM:(hh + 1) * A_HEAD_DIM]
    kb_ref[...] = k.astype(BF16)
    v = proj(2)
    for hh in range(A_HEADS):
        v32_ref[:, hh, :] = v[:, hh * LANES:(hh + 1) * LANES]
    if tk is None:
        vb_ref[...] = v.astype(BF16)
    else:
        vt = v.T.astype(BF16)
        for c in range(vb_ref.shape[0]):
            vb_ref[c] = vt[:, c * tk:(c + 1) * tk]

    lbl = lbl_ref[...]
    e = jnp.exp(lbl - jnp.max(lbl, axis=0, keepdims=True))
    lb = jnp.sum(e[:layer + 1], axis=0, keepdims=True) / jnp.sum(e, axis=0, keepdims=True)

    qb = proj(3)
    hq_ref[...] = (qb * _sigmoid(qb)).astype(BF16)
    f = lb + (1.0 - lb) * _sigmoid(proj(4))
    hk_ref[...] = (1.0 - f).astype(BF16)
    lf_ref[...] = jnp.log(f)
    hv_ref[...] = proj(5).astype(BF16)
    gb = proj(6)
    gt_ref[...] = (gb * _sigmoid(gb)).astype(BF16)


def _mixer_in(x2d, norm_g, w_in_bf, gq, gk, lb_logits, layer, tm, tk):
    t, d = x2d.shape
    w = A_WIDTH
    seg = (jnp.arange(w)[:, None] // A_HEAD_DIM == jnp.arange(w)[None, :] // A_HEAD_DIM).astype(BF16)
    row = lambda i: (i, 0)
    out_dt = [BF16, BF16, BF16, F32, F32, BF16, BF16, BF16, F32, BF16]
    out_shape = [jax.ShapeDtypeStruct((t, w), dt) for dt in out_dt]
    out_specs = [pl.BlockSpec((tm, w), row) for _ in out_dt]
    row3 = lambda i: (i, 0, 0)
    out_shape[3] = jax.ShapeDtypeStruct((t, 2 * A_HEADS, A_HEAD_DIM), F32)
    out_specs[3] = pl.BlockSpec((tm, 2 * A_HEADS, A_HEAD_DIM), row3)
    out_shape[4] = jax.ShapeDtypeStruct((t, A_HEADS, 2 * A_HEAD_DIM), F32)
    out_specs[4] = pl.BlockSpec((tm, A_HEADS, 2 * A_HEAD_DIM), row3)
    if tk is not None:
        out_shape[2] = jax.ShapeDtypeStruct((t // tk, w, tk), BF16)
        out_specs[2] = pl.BlockSpec((tm // tk, w, tk), lambda i: (i, 0, 0))
    return pl.pallas_call(
        functools.partial(_mixer_in_kernel, layer=layer, tk=tk),
        out_shape=out_shape,
        grid=(t // tm,),
        in_specs=[pl.BlockSpec((tm, d), row), _const_spec((1, d)), _const_spec(w_in_bf.shape),
                  _const_spec((1, w)), _const_spec((1, w)),
                  _const_spec(lb_logits.shape), _const_spec((w, w))],
        out_specs=out_specs,
        compiler_params=_cparams(("parallel",)),
        name="mixer_in",
    )(x2d, norm_g.reshape(1, d), w_in_bf,
      jnp.tile(gq, 2 * A_HEADS).reshape(1, w), jnp.tile(gk, 2 * A_HEADS).reshape(1, w),
      lb_logits, seg)


def _lambda(lam_ref, lam0):
    lv = lam_ref[...]
    e1 = jnp.exp(jnp.sum(lv[0:1] * lv[1:2], axis=-1, keepdims=True))
    e2 = jnp.exp(jnp.sum(lv[2:3] * lv[3:4], axis=-1, keepdims=True))
    return e1 - e2 + lam0


def _split_maps(k):
    first = lax.broadcasted_iota(I32, k.shape, 1) < A_HEAD_DIM
    zero = jnp.zeros_like(k)
    return jnp.where(first, k, zero), jnp.where(first, zero, k)


def _attn_prompt_kernel(lam_ref, bias_ref, q_ref, k_ref, vt_ref, o_ref,
                        sa_ref, sb_ref, pa_ref, pb_ref, acc_ref, *, tq, tk, lam0):
    assert tq == 2 * tk
    qi = pl.program_id(2)
    q = q_ref[0]
    s_bufs = (sa_ref, sb_ref)
    p_bufs = (pa_ref, pb_ref)

    half = tq // 2

    def phase(j, par, mx_cur, ml, masked, prefetch):
        s_ref, p_ref = s_bufs[par], p_bufs[par]
        sn_ref, pp_ref = s_bufs[1 - par], p_bufs[1 - par]
        vt = vt_ref[jnp.maximum(j - 1, 0)]
        lo = half if masked == 1 else 0
        mx_next = []
        if prefetch:
            k = k_ref[0, pl.ds(pl.multiple_of((j + 1) * tk, tk), tk), :]
            lo_next = half if masked == 0 else 0
            for mp, km in enumerate(_split_maps(k)):
                s = _dot_nt(km, q[lo_next:])
                sn_ref[mp, :, lo_next:] = s
                mx_next.append(jnp.max(s, axis=0, keepdims=True))
        new_ml = []
        for mp in range(2):
            m, l = ml[mp]
            pv = _dot(vt, pp_ref[mp])
            s = s_ref[mp, :, lo:]
            s_max = mx_cur[mp]
            if masked is not None:
                s = s + bias_ref[masked, :, lo:]
                s_max = jnp.max(s, axis=0, keepdims=True)
            m_new = jnp.maximum(m[:, lo:], s_max)
            alpha = jnp.exp2(m[:, lo:] - m_new)
            p = jnp.exp2(s - m_new)
            l_new = alpha * l[:, lo:] + jnp.sum(p, axis=0, keepdims=True)
            p = p.astype(BF16)
            if lo:
                keep = jnp.ones((1, lo), F32)
                m_new = jnp.concatenate([m[:, :lo], m_new], axis=1)
                l_new = jnp.concatenate([l[:, :lo], l_new], axis=1)
                alpha = jnp.concatenate([keep, alpha], axis=1)
                p = jnp.concatenate([jnp.zeros((tk, lo), BF16), p], axis=1)
            p_ref[mp] = p
            acc_ref[mp] = alpha * (acc_ref[mp] + pv)
            new_ml.append((m_new, l_new))
        return tuple(mx_next), tuple(new_ml)

    k0 = k_ref[0, pl.ds(0, tk), :]
    mx = []
    for mp, km in enumerate(_split_maps(k0)):
        s = _dot_nt(km, q)
        sa_ref[mp] = s
        mx.append(jnp.max(s, axis=0, keepdims=True))
    pb_ref[...] = jnp.zeros_like(pb_ref)
    acc_ref[...] = jnp.zeros_like(acc_ref)
    ml = tuple((jnp.full((1, tq), -jnp.inf, F32), jnp.zeros((1, tq), F32)) for _ in range(2))

    def pair(t, carry):
        mx_a, ml = carry
        mx_b, ml = phase(2 * t, 0, mx_a, ml, None, True)
        mx_a, ml = phase(2 * t + 1, 1, mx_b, ml, None, True)
        return mx_a, ml

    mx_a, ml = lax.fori_loop(0, qi, pair, (tuple(mx), ml))
    mx_b, ml = phase(2 * qi, 0, mx_a, ml, 0, True)
    _, ml = phase(2 * qi + 1, 1, mx_b, ml, 1, False)
    vt = vt_ref[2 * qi + 1]
    (_, l1), (_, l2) = ml
    o1 = (acc_ref[0] + _dot(vt, pb_ref[0])) / l1
    o2 = (acc_ref[1] + _dot(vt, pb_ref[1])) / l2
    o_ref[0] = (o1 - _lambda(lam_ref, lam0) * o2).T.astype(o_ref.dtype)


def _attn_prompt(lam_vec, qs, kb, vt, lam0, tq, tk):
    b, s, w = qs.shape
    nkb = s // tk
    blk = lambda bi, hi, qi: (bi, qi, hi)
    kc = (jnp.arange(2 * tk) // CHUNK).reshape(2, tk, 1)
    qc = (jnp.arange(tq) // CHUNK).reshape(1, 1, tq)
    bias = jnp.where(kc <= qc, 0.0, -jnp.inf).astype(F32)
    return pl.pallas_call(
        functools.partial(_attn_prompt_kernel, tq=tq, tk=tk, lam0=lam0),
        out_shape=jax.ShapeDtypeStruct((b, s, w), F32),
        grid=(b, A_HEADS, s // tq),
        in_specs=[_const_spec(lam_vec.shape), _const_spec(bias.shape),
                  pl.BlockSpec((1, tq, LANES), blk),
                  pl.BlockSpec((1, s, LANES), lambda bi, hi, qi: (bi, 0, hi)),
                  pl.BlockSpec((nkb, LANES, tk), lambda bi, hi, qi: (bi, hi, 0))],
        out_specs=pl.BlockSpec((1, tq, LANES), blk),
        scratch_shapes=[pltpu.VMEM((2, tk, tq), F32), pltpu.VMEM((2, tk, tq), F32),
                        pltpu.VMEM((2, tk, tq), BF16), pltpu.VMEM((2, tk, tq), BF16),
                        pltpu.VMEM((2, LANES, tq), F32)],
        compiler_params=_cparams(("parallel", "parallel", "arbitrary")),
        name="attn_prompt",
    )(lam_vec, bias, qs, kb, vt)


def _attn_sample_kernel(lam_ref, q_ref, kc_ref, vc_ref, kn_ref, vn_ref, o_ref, *, lam0):
    q = q_ref[0]
    vc = vc_ref[0]
    vn = vn_ref[0]
    outs = []
    for kcm, knm in zip(_split_maps(kc_ref[0]), _split_maps(kn_ref[0])):
        sc = _dot_nt(q, kcm)
        sn = _dot_nt(q, knm)
        m = jnp.maximum(jnp.max(sc, axis=-1, keepdims=True), jnp.max(sn, axis=-1, keepdims=True))
        pc = jnp.exp2(sc - m)
        pn = jnp.exp2(sn - m)
        l = jnp.sum(pc, axis=-1, keepdims=True) + jnp.sum(pn, axis=-1, keepdims=True)
        outs.append((_dot(pc.astype(BF16), vc) + _dot(pn.astype(BF16), vn)) / l)
    o_ref[0] = (outs[0] - _lambda(lam_ref, lam0) * outs[1]).astype(o_ref.dtype)


def _attn_sample(lam_vec, qs, kc, vc, kn, vn, lam0):
    b, t, w = qs.shape
    past = kc.shape[1]
    blk = lambda bi, hi: (bi, 0, hi)
    return pl.pallas_call(
        functools.partial(_attn_sample_kernel, lam0=lam0),
        out_shape=jax.ShapeDtypeStruct((b, t, w), F32),
        grid=(b, A_HEADS),
        in_specs=[_const_spec(lam_vec.shape), pl.BlockSpec((1, t, LANES), blk),
                  pl.BlockSpec((1, past, LANES), blk), pl.BlockSpec((1, past, LANES), blk),
                  pl.BlockSpec((1, t, LANES), blk), pl.BlockSpec((1, t, LANES), blk)],
        out_specs=pl.BlockSpec((1, t, LANES), blk),
        compiler_params=_cparams(("parallel", "parallel")),
        name="attn_sample",
    )(lam_vec, qs, kc, vc, kn, vn)


def _cumsum_rows(x):
    c = x.shape[0]
    row = lax.broadcasted_iota(I32, x.shape, 0)
    d = 1
    while d < c:
        x = x + jnp.where(row >= d, pltpu.roll(x, d, axis=0), 0.0)
        d *= 2
    return x


def _block_ref_rows(cum, m):
    c, n = cum.shape
    if m >= 8:
        r = cum.reshape(c // (2 * m), 2 * m, n)[:, m - 1:m, :]
        return jnp.broadcast_to(r, (c // (2 * m), 2 * m, n)).reshape(c, n)
    pos = lax.broadcasted_iota(I32, cum.shape, 0) % (2 * m)
    out = cum
    for p in range(2 * m):
        shift = p - (m - 1)
        if shift != 0:
            out = jnp.where(pos == p, pltpu.roll(cum, shift % c, axis=0), out)
    return out


def _hgrn_kernel(hq_ref, hk_ref, hv_ref, lf_ref, s0_ref, o_ref, st_ref, stt_ref, *, c):
    ci = pl.program_id(1)

    @pl.when(ci == 0)
    def _():
        for h in range(B_HEADS):
            stt_ref[h] = s0_ref[0, h].T

    row = lax.broadcasted_iota(I32, (c, B_DIM), 0)
    r2 = lax.broadcasted_iota(I32, (c, c), 0)
    c2 = lax.broadcasted_iota(I32, (c, c), 1)
    levels = []
    m = c // 2
    while m >= 1:
        levels.append((m, (row % (2 * m)) >= m, (r2 // (2 * m)) == (c2 // (2 * m))))
        m //= 2
    for h in range(B_HEADS):
        sl = slice(h * B_DIM, (h + 1) * B_DIM)
        hq = hq_ref[0, :, sl].astype(F32)
        hk = hk_ref[0, :, sl].astype(F32)
        hv = hv_ref[0, :, sl]
        cum = _cumsum_rows(lf_ref[0, :, sl])
        last = cum[c - 1:c, :]

        a = None
        for m, upper, same_block in levels:
            d = cum - _block_ref_rows(cum, m)
            qd = hq * jnp.exp(jnp.where(upper, d, -jnp.inf))
            kd = hk * jnp.exp(jnp.where(upper, -jnp.inf, -d))
            p = _dot_nt(qd.astype(BF16), kd.astype(BF16))
            a = jnp.where(same_block, p, 0.0 if a is None else a)

        stt = stt_ref[h]
        o = _dot_nt((hq * jnp.exp(cum)).astype(BF16), stt.astype(BF16))
        o = o + _dot(a.astype(BF16), hv)
        o = o + jnp.sum(hq * hk, axis=-1, keepdims=True) * hv.astype(F32)
        o_ref[0, :, sl] = o.astype(o_ref.dtype)

        kl = (hk * jnp.exp(last - cum)).astype(BF16)
        stt_ref[h] = stt * jnp.exp(last) + _dot_tn(hv, kl)

    @pl.when(ci == pl.num_programs(1) - 1)
    def _():
        for h in range(B_HEADS):
            st_ref[0, h] = stt_ref[h].T


def _hgrn(hq, hk, hv, lf, s0, c):
    b, s, w = hq.shape
    blk = lambda bi, ci: (bi, ci, 0)
    sblk = lambda bi, ci: (bi, 0, 0, 0)
    return pl.pallas_call(
        functools.partial(_hgrn_kernel, c=c),
        out_shape=[jax.ShapeDtypeStruct((b, s, w), F32),
                   jax.ShapeDtypeStruct((b, B_HEADS, B_DIM, B_DIM), F32)],
        grid=(b, s // c),
        in_specs=[pl.BlockSpec((1, c, w), blk)] * 4 + [pl.BlockSpec((1, B_HEADS, B_DIM, B_DIM), sblk)],
        out_specs=[pl.BlockSpec((1, c, w), blk), pl.BlockSpec((1, B_HEADS, B_DIM, B_DIM), sblk)],
        scratch_shapes=[pltpu.VMEM((B_HEADS, B_DIM, B_DIM), F32)],
        compiler_params=_cparams(("parallel", "arbitrary")),
        name="hgrn",
    )(hq, hk, hv, lf, s0)


def _mem_kv_kernel(m_ref, g_ref, wk_ref, wv_ref, gk_ref, k_ref, v_ref):
    hm = _rms(m_ref[...], g_ref[...]).astype(BF16)
    k = _dot(hm, wk_ref[...])
    gk = gk_ref[...]
    hd = k.shape[-1] // X_HEADS
    for h in range(X_HEADS):
        sl = slice(h * hd, (h + 1) * hd)
        k_ref[:, sl] = _rms(k[:, sl], gk)
    v_ref[...] = _dot(hm, wv_ref[...])


def _mem_kv(mem2d, norm_g, w_mk_bf, w_mv_bf, xk_g, tm):
    t, d = mem2d.shape
    row = lambda i: (i, 0)
    return pl.pallas_call(
        _mem_kv_kernel,
        out_shape=[jax.ShapeDtypeStruct((t, d), F32)] * 2,
        grid=(t // tm,),
        in_specs=[pl.BlockSpec((tm, d), row), _const_spec((1, d)), _const_spec(w_mk_bf.shape),
                  _const_spec(w_mv_bf.shape), _const_spec((1, xk_g.shape[-1]))],
        out_specs=[pl.BlockSpec((tm, d), row)] * 2,
        compiler_params=_cparams(("parallel",)),
        name="mem_kv",
    )(mem2d, norm_g.reshape(1, d), w_mk_bf, w_mv_bf, xk_g.reshape(1, -1))


def _post_kernel(x_ref, oa_ref, ob_ref, gt_ref, sg_ref, gg_ref, wo_ref,
                 nx_ref, wq_ref, xq_ref, mk_ref, mv_ref, wxo_ref,
                 nf_ref, wr_ref, br_ref,
                 x2_ref, xs_ref, rloc_ref, gate_ref, cnt_ref, *, lam0):
    ya = []
    yb = []
    for h in range(A_HEADS):
        sl = slice(h * LANES, (h + 1) * LANES)
        ya.append((_rms(oa_ref[:, sl], sg_ref[...]) * (1.0 - lam0)).astype(BF16))
        yb.append((_rms(ob_ref[:, sl], gg_ref[...]) * gt_ref[:, sl].astype(F32)).astype(BF16))
    y = jnp.concatenate(ya + yb, axis=-1)
    x1 = x_ref[...] + _dot(y, wo_ref[...])

    q = _dot(_rms(x1, nx_ref[...]).astype(BF16), wq_ref[...])
    hd = q.shape[-1] // X_HEADS
    os_ = []
    for h in range(X_HEADS):
        sl = slice(h * hd, (h + 1) * hd)
        qh = _rms(q[:, sl], xq_ref[...]).astype(BF16)
        s = _dot_nt(qh, mk_ref[0, :, sl]) * (hd ** -0.5)
        p = jnp.exp(s - jnp.max(s, axis=-1, keepdims=True))
        p = p / jnp.sum(p, axis=-1, keepdims=True)
        os_.append(_dot(p.astype(BF16), mv_ref[0, :, sl]).astype(BF16))
    x2 = x1 + _dot(jnp.concatenate(os_, axis=-1), wxo_ref[...])
    x2_ref[...] = x2

    h2 = _rms(x2, nf_ref[...])
    w = h2.shape[0]
    h_hi = h2.astype(BF16)
    h_lo = (h2 - h_hi.astype(F32)).astype(BF16)
    wr = wr_ref[...]
    w_hi = wr.astype(BF16)
    w_lo = (wr - w_hi.astype(F32)).astype(BF16)
    a = _dot_nt(jnp.concatenate([w_hi, w_lo], axis=0), h_hi)
    logits = a[:N_EXPERTS] + a[N_EXPERTS:] + _dot_nt(w_hi, h_lo) + br_ref[...]
    erow = lax.broadcasted_iota(I32, logits.shape, 0)
    picks, vals = [], []
    for k in range(TOP_K):
        mx = jnp.max(logits, axis=0, keepdims=True)
        ix = jnp.min(jnp.where(logits == mx, erow, N_EXPERTS), axis=0, keepdims=True)
        pick = erow == ix
        picks.append(pick)
        vals.append(mx)
        logits = jnp.where(pick, -jnp.inf, logits)
    ex = [jnp.exp(v - vals[0]) for v in vals]
    den = ex[0] + ex[1] + ex[2] + ex[3]

    onehot = jnp.where(picks[0] | picks[1] | picks[2] | picks[3], 1.0, 0.0)
    earlier = (lax.broadcasted_iota(I32, (w, w), 0) < lax.broadcasted_iota(I32, (w, w), 1))
    rank = _dot(onehot.astype(BF16), jnp.where(earlier, 1.0, 0.0).astype(BF16))
    cnt = jnp.sum(onehot, axis=1, keepdims=True).astype(I32)
    pcnt = ((cnt + (ROW_BLOCK - 1)) // ROW_BLOCK) * ROW_BLOCK
    pc_b = jnp.broadcast_to(pcnt.astype(F32), onehot.shape)
    seg_off = _cumsum_rows(pc_b) - pc_b
    cap = xs_ref.shape[0]
    riota = lax.broadcasted_iota(I32, (cap, w), 0)
    sel = None
    lane8 = lax.broadcasted_iota(I32, rloc_ref.shape, 0)
    rloc_out = jnp.zeros(rloc_ref.shape, I32)
    gate_out = jnp.zeros(gate_ref.shape, F32)
    for k in range(TOP_K):
        rloc = jnp.sum(jnp.where(picks[k], seg_off + rank, 0.0), axis=0, keepdims=True).astype(I32)
        hit = riota == rloc
        sel = hit if sel is None else (sel | hit)
        rloc_out = jnp.where(lane8 == k, rloc, rloc_out)
        gate_out = jnp.where(lane8 == k, ex[k] / den, gate_out)
    xs_ref[...] = _dot(jnp.where(sel, 1.0, 0.0).astype(BF16), h_hi)
    rloc_ref[...] = rloc_out
    gate_ref[...] = gate_out
    cnt_ref[...] = jnp.broadcast_to(cnt, cnt_ref.shape)


def _window_cap(w):
    cap = w * TOP_K + N_EXPERTS * (ROW_BLOCK - 1) + 2 * ROW_BLOCK
    return -(-cap // LANES) * LANES


def _post(x2d, oa, ob, gt, subln_g, gn_g, w_out_bf, norm_x_g, w_xq_bf, xq_g, mk_bf, mv_bf,
          w_xo_bf, norm_ffn_g, w_router, b_router, lam0, tm, rows_per_batch):
    t, d = x2d.shape
    n_win = t // tm
    cap = _window_cap(tm)
    row = lambda i: (i, 0)
    tiles_per_batch = rows_per_batch // tm
    mem = lambda i: (i // tiles_per_batch, 0, 0)
    n_mem = mk_bf.shape[1]
    vec = lambda a: a.reshape(1, -1)
    return pl.pallas_call(
        functools.partial(_post_kernel, lam0=lam0),
        out_shape=[jax.ShapeDtypeStruct((t, d), F32),
                   jax.ShapeDtypeStruct((n_win * cap, d), F32),
                   jax.ShapeDtypeStruct((n_win * 8, tm), I32),
                   jax.ShapeDtypeStruct((n_win * 8, tm), F32),
                   jax.ShapeDtypeStruct((n_win * N_EXPERTS, LANES), I32)],
        grid=(n_win,),
        in_specs=[pl.BlockSpec((tm, d), row), pl.BlockSpec((tm, A_WIDTH), row),
                  pl.BlockSpec((tm, A_WIDTH), row), pl.BlockSpec((tm, A_WIDTH), row),
                  _const_spec((1, LANES)), _const_spec((1, LANES)), _const_spec(w_out_bf.shape),
                  _const_spec((1, d)), _const_spec(w_xq_bf.shape), _const_spec((1, xq_g.shape[-1])),
                  pl.BlockSpec((1, n_mem, d), mem), pl.BlockSpec((1, n_mem, d), mem),
                  _const_spec(w_xo_bf.shape),
                  _const_spec((1, d)), _const_spec((N_EXPERTS, d)), _const_spec((N_EXPERTS, 1))],
        out_specs=[pl.BlockSpec((tm, d), row), pl.BlockSpec((cap, d), row),
                   pl.BlockSpec((8, tm), row), pl.BlockSpec((8, tm), row),
                   pl.BlockSpec((N_EXPERTS, LANES), row)],
        compiler_params=_cparams(("parallel",)),
        name="post",
    )(x2d, oa, ob, gt, vec(subln_g), vec(gn_g), w_out_bf, vec(norm_x_g), w_xq_bf, vec(xq_g),
      mk_bf, mv_bf, w_xo_bf, vec(norm_ffn_g), w_router.T, b_router.reshape(N_EXPERTS, 1))


def _moe_expert_kernel(te_ref, tf_ref, tv_ref, g0_ref, gn_ref, sd_ref, xs_hbm,
                       wgu_ref, bgu_ref, wd_ref, bd_ref, ys_hbm,
                       xbuf, ybuf, wgu_bf, wd_bf, gsem, ssem, *, tm):
    i = pl.program_id(0)
    last = pl.num_programs(0) - 1
    slot = i % 2
    nb = tm // ROW_BLOCK
    valid = tv_ref[i] == 1
    next_valid = jnp.logical_and(i < last, tv_ref[jnp.minimum(i + 1, last)] == 1)

    def gather(tab_ref, s):
        for j in range(nb):
            r = pl.multiple_of(tab_ref[0, 0, j], ROW_BLOCK)
            pltpu.make_async_copy(xs_hbm.at[pl.ds(r, ROW_BLOCK)],
                                  xbuf.at[s, pl.ds(j * ROW_BLOCK, ROW_BLOCK)], gsem.at[s]).start()

    def scatter(s):
        for j in range(nb):
            r = pl.multiple_of(sd_ref[0, 0, j], ROW_BLOCK)
            pltpu.make_async_copy(ybuf.at[s, pl.ds(j * ROW_BLOCK, ROW_BLOCK)],
                                  ys_hbm.at[pl.ds(r, ROW_BLOCK)], ssem.at[s]).start()

    def wait_gather(s):
        pltpu.make_async_copy(xs_hbm.at[pl.ds(0, tm)], xbuf.at[s], gsem.at[s]).wait()

    def wait_scatter(s):
        pltpu.make_async_copy(ybuf.at[s], ys_hbm.at[pl.ds(0, tm)], ssem.at[s]).wait()

    @pl.when(jnp.logical_and(i == 0, valid))
    def _():
        gather(g0_ref, 0)

    @pl.when(jnp.logical_and(valid, tf_ref[i] == 1))
    def _():
        wgu_bf[...] = wgu_ref[0].astype(BF16)
        wd_bf[...] = wd_ref[0].astype(BF16)

    @pl.when(valid)
    def _():
        wait_gather(slot)

        @pl.when(next_valid)
        def _():
            gather(gn_ref, 1 - slot)

        de = wd_bf.shape[0]
        gu = _dot(xbuf[slot].astype(BF16), wgu_bf[...]) + bgu_ref[0]
        gate = jnp.minimum(gu[:, :de], SWIGLU_LIMIT)
        up = jnp.clip(gu[:, de:], -SWIGLU_LIMIT, SWIGLU_LIMIT)
        act = (up + 1.0) * gate * _sigmoid(SWIGLU_ALPHA * gate)
        ybuf[slot] = _dot(act.astype(BF16), wd_bf[...]) + bd_ref[0]

        @pl.when(i > 0)
        def _():
            wait_scatter(1 - slot)

        scatter(slot)

        @pl.when(jnp.logical_not(next_valid))
        def _():
            wait_scatter(slot)


def _moe_expert(tables, xs, w_gate_up, b_gate_up, w_down, b_down, tm):
    tile_expert, tile_first, tile_valid, gsrc, sdst = tables
    nt = tile_expert.shape[0]
    nb = tm // ROW_BLOCK
    d = xs.shape[1]
    ne, _, dgu = w_gate_up.shape
    de = w_down.shape[1]
    smem = lambda f: pl.BlockSpec((1, 1, nb), f, memory_space=pltpu.SMEM)
    wmap = lambda i, te, tf, tv: (te[i], 0, 0)
    grid_spec = pltpu.PrefetchScalarGridSpec(
        num_scalar_prefetch=3,
        grid=(nt,),
        in_specs=[smem(lambda i, te, tf, tv: (0, 0, 0)),
                  smem(lambda i, te, tf, tv: (jnp.minimum(i + 1, nt - 1), 0, 0)),
                  smem(lambda i, te, tf, tv: (i, 0, 0)),
                  pl.BlockSpec(memory_space=pl.ANY),
                  pl.BlockSpec((1, d, dgu), wmap), pl.BlockSpec((1, 1, dgu), wmap),
                  pl.BlockSpec((1, de, d), wmap), pl.BlockSpec((1, 1, d), wmap)],
        out_specs=pl.BlockSpec(memory_space=pl.ANY),
        scratch_shapes=[pltpu.VMEM((2, tm, d), F32), pltpu.VMEM((2, tm, d), F32),
                        pltpu.VMEM((d, dgu), BF16), pltpu.VMEM((de, d), BF16),
                        pltpu.SemaphoreType.DMA((2,)), pltpu.SemaphoreType.DMA((2,))],
    )
    return pl.pallas_call(
        functools.partial(_moe_expert_kernel, tm=tm),
        out_shape=jax.ShapeDtypeStruct(xs.shape, F32),
        grid_spec=grid_spec,
        input_output_aliases={6: 0},
        compiler_params=_cparams(("arbitrary",)),
        name="moe_expert",
    )(tile_expert, tile_first, tile_valid, gsrc.reshape(nt, 1, nb), gsrc.reshape(nt, 1, nb),
      sdst.reshape(nt, 1, nb), xs, w_gate_up, b_gate_up.reshape(ne, 1, dgu), w_down,
      b_down.reshape(ne, 1, d))


def _moe_combine_kernel(x_ref, ys_ref, rloc_ref, gate_ref, o_ref):
    cap = ys_ref.shape[0]
    w = x_ref.shape[0]
    riota = lax.broadcasted_iota(I32, (cap, w), 0)
    gt = jnp.zeros((cap, w), F32)
    for k in range(TOP_K):
        gt = jnp.where(riota == rloc_ref[k:k + 1, :], gate_ref[k:k + 1, :], gt)
    o_ref[...] = x_ref[...] + _dot_tn(gt.astype(BF16), ys_ref[...].astype(BF16))


def _moe_combine(x2, ys, rloc, gate, tm):
    t, d = x2.shape
    cap = _window_cap(tm)
    row = lambda i: (i, 0)
    return pl.pallas_call(
        _moe_combine_kernel,
        out_shape=jax.ShapeDtypeStruct((t, d), F32),
        grid=(t // tm,),
        in_specs=[pl.BlockSpec((tm, d), row), pl.BlockSpec((cap, d), row),
                  pl.BlockSpec((8, tm), row), pl.BlockSpec((8, tm), row)],
        out_specs=pl.BlockSpec((tm, d), row),
        compiler_params=_cparams(("parallel",)),
        name="moe_combine",
    )(x2, ys, rloc, gate)


def _moe_tables(cnt, cap, n_picks, tm_e):
    n_win = cnt.shape[0]
    bpt = tm_e // ROW_BLOCK
    nt = (n_picks // ROW_BLOCK + n_win * N_EXPERTS) // bpt + N_EXPERTS
    blk = (cnt + ROW_BLOCK - 1) // ROW_BLOCK
    seg_blk = jnp.cumsum(blk, axis=1) - blk
    cum_incl = jnp.cumsum(blk, axis=0)
    cum_excl = cum_incl - blk
    tot = cum_incl[-1]
    tiles = (tot + bpt - 1) // bpt
    tile_end = jnp.cumsum(tiles)
    tile_beg = tile_end - tiles
    ti = jnp.arange(nt, dtype=I32)
    te = jnp.minimum(jnp.sum((ti[:, None] >= tile_end[None, :]).astype(I32), axis=1), N_EXPERTS - 1)
    tv = ti < tile_end[-1]
    tf = jnp.concatenate([jnp.ones((1,), bool), te[1:] != te[:-1]])
    v = (ti - tile_beg[te])[:, None] * bpt + jnp.arange(bpt, dtype=I32)[None, :]
    ok = tv[:, None] & (v < tot[te][:, None])
    widx = jnp.sum((cum_incl.T[te][:, None, :] <= v[:, :, None]).astype(I32), axis=-1)
    onehot = jnp.minimum(widx, n_win - 1)[:, :, None] == jnp.arange(n_win, dtype=I32)
    base = (jnp.arange(n_win, dtype=I32) * cap)[None, :] + (seg_blk.T[te] - cum_excl.T[te]) * ROW_BLOCK
    row = jnp.sum(jnp.where(onehot, base[:, None, :], 0), axis=-1) + v * ROW_BLOCK
    assert n_win >= bpt
    spare = jnp.arange(bpt, dtype=I32)[None, :] * cap + (cap - 2 * ROW_BLOCK)
    gsrc = jnp.where(ok, row, cap - ROW_BLOCK).astype(I32)
    sdst = jnp.where(ok, row, spare).astype(I32)
    return te.astype(I32), tf.astype(I32), tv.astype(I32), gsrc, sdst


def _moe(x2, xs, rloc, gate, cnt, w_gate_up, b_gate_up, w_down, b_down, tm_w, tm_e):
    n_win = x2.shape[0] // tm_w
    cnt2 = cnt.reshape(n_win, N_EXPERTS, LANES)[:, :, 0]
    tables = _moe_tables(cnt2, _window_cap(tm_w), x2.shape[0] * TOP_K, tm_e)
    ys = _moe_expert(tables, xs, w_gate_up, b_gate_up, w_down, b_down, tm_e)
    return _moe_combine(x2, ys, rloc, gate, tm_w)


def _pick(n, pref):
    return pref if n % pref == 0 else n


def _group(x, mk_bf, mv_bf, attn_fn, attn_tk, s0, l, p, bf, lam0):
    b, t, d = x.shape
    x2d = x.reshape(b * t, d)
    (qs, kb, vb, k32, v32, hq, hk, hv, lf, gt) = _mixer_in(
        x2d, p["norm_mix_g"][l], bf["w_in"][l], p["a_qnorm_g"][l], p["a_knorm_g"][l],
        p["hgrn_lb_logits"], l, _pick(b * t, 512), attn_tk)
    r3 = lambda a: a.reshape(b, t, -1)
    oa = attn_fn(r3(qs), r3(kb), vb if attn_tk else r3(vb))
    ob, s_new = _hgrn(r3(hq), r3(hk), r3(hv), r3(lf), s0, _pick(t, 128))
    tm_w = _pick(t, 512)
    tm_e = 512 if b * t >= 4096 else 64
    x2, xs, rloc, gate, cnt = _post(
        x2d, oa.reshape(b * t, -1), ob.reshape(b * t, -1), gt, p["a_subln_g"][l], p["b_gnorm_g"][l],
        bf["w_out"][l], p["norm_x_g"][l], bf["w_xq"][l], p["xq_norm_g"][l], mk_bf, mv_bf,
        bf["w_xo"][l], p["norm_ffn_g"][l], p["w_router"][l], p["b_router"][l], lam0,
        tm_w, t)
    y = _moe(x2, xs, rloc, gate, cnt, p["w_gate_up"][l], p["b_gate_up"][l], p["w_down"][l],
             p["b_down"][l], tm_w, tm_e)
    return (y.reshape(b, t, d), k32.reshape(b, t, 2 * A_HEADS, A_HEAD_DIM),
            v32.reshape(b, t, A_HEADS, 2 * A_HEAD_DIM), s_new)


def kernel(x_prompt, x_sample, mem_prompt, cache_attn_k, cache_attn_v, state_hgrn, cache_mem_k, cache_mem_v, norm_mix_g, w_in, a_qnorm_g, a_knorm_g, lambda_q1, lambda_k1, lambda_q2, lambda_k2, a_subln_g, hgrn_lb_logits, b_gnorm_g, w_out, norm_x_g, norm_mem_g, w_xq, w_mk, w_mv, xq_norm_g, xk_norm_g, w_xo, norm_ffn_g, w_router, b_router, w_gate_up, b_gate_up, w_down, b_down):
    p = dict(norm_mix_g=norm_mix_g, a_qnorm_g=a_qnorm_g, a_knorm_g=a_knorm_g, a_subln_g=a_subln_g,
             hgrn_lb_logits=hgrn_lb_logits, b_gnorm_g=b_gnorm_g, norm_x_g=norm_x_g,
             xq_norm_g=xq_norm_g, norm_ffn_g=norm_ffn_g, w_router=w_router, b_router=b_router,
             w_gate_up=w_gate_up, b_gate_up=b_gate_up, w_down=w_down, b_down=b_down)
    bf = dict(w_in=w_in.astype(BF16), w_out=w_out.astype(BF16), w_xq=w_xq.astype(BF16),
              w_xo=w_xo.astype(BF16))
    depth = w_in.shape[0]
    b, s, d = x_prompt.shape
    db, ds, _ = x_sample.shape
    n_mem = mem_prompt.shape[1]
    xp, xs = x_prompt, x_sample
    outs = [[] for _ in range(8)]
    for l in range(depth):
        lam0 = 0.8 - 0.6 * math.exp(-0.3 * l)
        lam_vec = jnp.stack([lambda_q1[l], lambda_k1[l], lambda_q2[l], lambda_k2[l]])

        mk, mv = _mem_kv(mem_prompt.reshape(b * n_mem, d), norm_mem_g[l], w_mk[l].astype(BF16),
                         w_mv[l].astype(BF16), xk_norm_g[l], _pick(b * n_mem, 512))
        tk = 4 * LANES
        attn_p = lambda q, k, vt: _attn_prompt(lam_vec, q, k, vt, lam0, 2 * tk, tk)
        s0 = jnp.zeros((b, B_HEADS, B_DIM, B_DIM), F32)
        xp, kp, vp, sp = _group(xp, mk.reshape(b, n_mem, d).astype(BF16),
                                mv.reshape(b, n_mem, d).astype(BF16), attn_p, tk, s0, l, p, bf, lam0)

        kc = cache_attn_k[l].reshape(db, -1, A_WIDTH).astype(BF16)
        vc = cache_attn_v[l].reshape(db, -1, A_WIDTH).astype(BF16)
        attn_s = lambda q, k, v: _attn_sample(lam_vec, q, kc, vc, k, v, lam0)
        xs, ks, vs, ss = _group(xs, cache_mem_k[l].reshape(db, n_mem, d).astype(BF16),
                                cache_mem_v[l].reshape(db, n_mem, d).astype(BF16), attn_s, None,
                                state_hgrn[l], l, p, bf, lam0)
        for lst, val in zip(outs, (kp, vp, sp, mk.reshape(b, n_mem, X_HEADS, -1),
                                   mv.reshape(b, n_mem, X_HEADS, -1), ks, vs, ss)):
            lst.append(val)
    return (xp, xs) + tuple(jnp.stack(o) for o in outs)
```

```python
import functools
import math

import jax
import jax.numpy as jnp
from jax import lax
from jax.experimental import pallas as pl
from jax.experimental.pallas import tpu as pltpu

F32 = jnp.float32
BF16 = jnp.bfloat16
I32 = jnp.int32

EPS = 1e-6
CHUNK = 64
A_HEADS = 4
A_HEAD_DIM = 64
A_WIDTH = A_HEADS * 2 * A_HEAD_DIM
B_HEADS = 4
B_DIM = 128
X_HEADS = 4
N_EXPERTS = 32
TOP_K = 4
SWIGLU_LIMIT = 7.0
SWIGLU_ALPHA = 1.702

LANES = 128
ROW_BLOCK = 8
VMEM_LIMIT = 56 * 1024 * 1024


def _cparams(sem):
    return pltpu.CompilerParams(dimension_semantics=sem, vmem_limit_bytes=VMEM_LIMIT)


def _const_spec(shape):
    zeros = (0,) * len(shape)
    return pl.BlockSpec(shape, lambda *_: zeros)


def _rms(x, g):
    return x * lax.rsqrt(jnp.mean(x * x, axis=-1, keepdims=True) + EPS) * g


def _dot(a, b):
    return jnp.dot(a, b, preferred_element_type=F32)


def _dot_nt(a, b):
    return lax.dot_general(a, b, (((1,), (1,)), ((), ())), preferred_element_type=F32)


def _dot_tn(a, b):
    return lax.dot_general(a, b, (((0,), (0,)), ((), ())), preferred_element_type=F32)


def _sigmoid(x):
    return 1.0 / (1.0 + jnp.exp(-x))


def _mixer_in_kernel(x_ref, g_ref, w_ref, gq_ref, gk_ref, lbl_ref, seg_ref,
                     qs_ref, kb_ref, vb_ref, k32_ref, v32_ref,
                     hq_ref, hk_ref, hv_ref, lf_ref, gt_ref, *, layer, tk):
    hb = _rms(x_ref[...], g_ref[...]).astype(BF16)

    def proj(j):
        return _dot(hb, w_ref[:, j * A_WIDTH:(j + 1) * A_WIDTH])

    seg = seg_ref[...]

    def head_norm(a, g):
        ss = _dot((a * a).astype(BF16), seg)
        return a * lax.rsqrt(ss * (1.0 / A_HEAD_DIM) + EPS) * g

    q = head_norm(proj(0), gq_ref[...])
    qs_ref[...] = (q * (A_HEAD_DIM ** -0.5 * math.log2(math.e))).astype(BF16)
    k = head_norm(proj(1), gk_ref[...])
    for hh in range(2 * A_HEADS):
        k32_ref[:, hh, :] = k[:, hh * A_HEAD_DIM:(hh + 1) * A_HEAD_DIM]
    kb_ref[...] = k.astype(BF16)
    v = proj(2)
    for hh in range(A_HEADS):
        v32_ref[:, hh, :] = v[:, hh * LANES:(hh + 1) * LANES]
    if tk is None:
        vb_ref[...] = v.astype(BF16)
    else:
        vt = v.T.astype(BF16)
        for c in range(vb_ref.shape[0]):
            vb_ref[c] = vt[:, c * tk:(c + 1) * tk]

    lbl = lbl_ref[...]
    e = jnp.exp(lbl - jnp.max(lbl, axis=0, keepdims=True))
    lb = jnp.sum(e[:layer + 1], axis=0, keepdims=True) / jnp.sum(e, axis=0, keepdims=True)

    qb = proj(3)
    hq_ref[...] = (qb * _sigmoid(qb)).astype(BF16)
    f = lb + (1.0 - lb) * _sigmoid(proj(4))
    hk_ref[...] = (1.0 - f).astype(BF16)
    lf_ref[...] = jnp.log(f)
    hv_ref[...] = proj(5).astype(BF16)
    gb = proj(6)
    gt_ref[...] = (gb * _sigmoid(gb)).astype(BF16)


def _mixer_in(x2d, norm_g, w_in_bf, gq, gk, lb_logits, layer, tm, tk):
    t, d = x2d.shape
    w = A_WIDTH
    seg = (jnp.arange(w)[:, None] // A_HEAD_DIM == jnp.arange(w)[None, :] // A_HEAD_DIM).astype(BF16)
    row = lambda i: (i, 0)
    out_dt = [BF16, BF16, BF16, F32, F32, BF16, BF16, BF16, F32, BF16]
    out_shape = [jax.ShapeDtypeStruct((t, w), dt) for dt in out_dt]
    out_specs = [pl.BlockSpec((tm, w), row) for _ in out_dt]
    row3 = lambda i: (i, 0, 0)
    out_shape[3] = jax.ShapeDtypeStruct((t, 2 * A_HEADS, A_HEAD_DIM), F32)
    out_specs[3] = pl.BlockSpec((tm, 2 * A_HEADS, A_HEAD_DIM), row3)
    out_shape[4] = jax.ShapeDtypeStruct((t, A_HEADS, 2 * A_HEAD_DIM), F32)
    out_specs[4] = pl.BlockSpec((tm, A_HEADS, 2 * A_HEAD_DIM), row3)
    if tk is not None:
        out_shape[2] = jax.ShapeDtypeStruct((t // tk, w, tk), BF16)
        out_specs[2] = pl.BlockSpec((tm // tk, w, tk), lambda i: (i, 0, 0))
    return pl.pallas_call(
        functools.partial(_mixer_in_kernel, layer=layer, tk=tk),
        out_shape=out_shape,
        grid=(t // tm,),
        in_specs=[pl.BlockSpec((tm, d), row), _const_spec((1, d)), _const_spec(w_in_bf.shape),
                  _const_spec((1, w)), _const_spec((1, w)),
                  _const_spec(lb_logits.shape), _const_spec((w, w))],
        out_specs=out_specs,
        compiler_params=_cparams(("parallel",)),
        name="mixer_in",
    )(x2d, norm_g.reshape(1, d), w_in_bf,
      jnp.tile(gq, 2 * A_HEADS).reshape(1, w), jnp.tile(gk, 2 * A_HEADS).reshape(1, w),
      lb_logits, seg)


def _lambda(lam_ref, lam0):
    lv = lam_ref[...]
    e1 = jnp.exp(jnp.sum(lv[0:1] * lv[1:2], axis=-1, keepdims=True))
    e2 = jnp.exp(jnp.sum(lv[2:3] * lv[3:4], axis=-1, keepdims=True))
    return e1 - e2 + lam0


def _split_maps(k):
    first = lax.broadcasted_iota(I32, k.shape, 1) < A_HEAD_DIM
    zero = jnp.zeros_like(k)
    return jnp.where(first, k, zero), jnp.where(first, zero, k)


def _attn_prompt_kernel(lam_ref, bias_ref, q_ref, k_ref, vt_ref, o_ref,
                        sa_ref, sb_ref, pa_ref, pb_ref, acc_ref, *, tq, tk, lam0):
    assert tq == 2 * tk
    qi = pl.program_id(2)
    q = q_ref[0]
    s_bufs = (sa_ref, sb_ref)
    p_bufs = (pa_ref, pb_ref)

    half = tq // 2

    def phase(j, par, mx_cur, ml, masked, prefetch):
        s_ref, p_ref = s_bufs[par], p_bufs[par]
        sn_ref, pp_ref = s_bufs[1 - par], p_bufs[1 - par]
        vt = vt_ref[jnp.maximum(j - 1, 0)]
        lo = half if masked == 1 else 0
        mx_next = []
        if prefetch:
            k = k_ref[0, pl.ds(pl.multiple_of((j + 1) * tk, tk), tk), :]
            lo_next = half if masked == 0 else 0
            for mp, km in enumerate(_split_maps(k)):
                s = _dot_nt(km, q[lo_next:])
                sn_ref[mp, :, lo_next:] = s
                mx_next.append(jnp.max(s, axis=0, keepdims=True))
        new_ml = []
        for mp in range(2):
            m, l = ml[mp]
            pv = _dot(vt, pp_ref[mp])
            s = s_ref[mp, :, lo:]
            s_max = mx_cur[mp]
            if masked is not None:
                s = s + bias_ref[masked, :, lo:]
                s_max = jnp.max(s, axis=0, keepdims=True)
            m_new = jnp.maximum(m[:, lo:], s_max)
            alpha = jnp.exp2(m[:, lo:] - m_new)
            p = jnp.exp2(s - m_new)
            l_new = alpha * l[:, lo:] + jnp.sum(p, axis=0, keepdims=True)
            p = p.astype(BF16)
            if lo:
                keep = jnp.ones((1, lo), F32)
                m_new = jnp.concatenate([m[:, :lo], m_new], axis=1)
                l_new = jnp.concatenate([l[:, :lo], l_new], axis=1)
                alpha = jnp.concatenate([keep, alpha], axis=1)
                p = jnp.concatenate([jnp.zeros((tk, lo), BF16), p], axis=1)
            p_ref[mp] = p
            acc_ref[mp] = alpha * (acc_ref[mp] + pv)
            new_ml.append((m_new, l_new))
        return tuple(mx_next), tuple(new_ml)

    k0 = k_ref[0, pl.ds(0, tk), :]
    mx = []
    for mp, km in enumerate(_split_maps(k0)):
        s = _dot_nt(km, q)
        sa_ref[mp] = s
        mx.append(jnp.max(s, axis=0, keepdims=True))
    pb_ref[...] = jnp.zeros_like(pb_ref)
    acc_ref[...] = jnp.zeros_like(acc_ref)
    ml = tuple((jnp.full((1, tq), -jnp.inf, F32), jnp.zeros((1, tq), F32)) for _ in range(2))

    def pair(t, carry):
        mx_a, ml = carry
        mx_b, ml = phase(2 * t, 0, mx_a, ml, None, True)
        mx_a, ml = phase(2 * t + 1, 1, mx_b, ml, None, True)
        return mx_a, ml

    mx_a, ml = lax.fori_loop(0, qi, pair, (tuple(mx), ml))
    mx_b, ml = phase(2 * qi, 0, mx_a, ml, 0, True)
    _, ml = phase(2 * qi + 1, 1, mx_b, ml, 1, False)
    vt = vt_ref[2 * qi + 1]
    (_, l1), (_, l2) = ml
    o1 = (acc_ref[0] + _dot(vt, pb_ref[0])) / l1
    o2 = (acc_ref[1] + _dot(vt, pb_ref[1])) / l2
    o_ref[0] = (o1 - _lambda(lam_ref, lam0) * o2).T.astype(o_ref.dtype)


def _attn_prompt(lam_vec, qs, kb, vt, lam0, tq, tk):
    b, s, w = qs.shape
    nkb = s // tk
    blk = lambda bi, hi, qi: (bi, qi, hi)
    kc = (jnp.arange(2 * tk) // CHUNK).reshape(2, tk, 1)
    qc = (jnp.arange(tq) // CHUNK).reshape(1, 1, tq)
    bias = jnp.where(kc <= qc, 0.0, -jnp.inf).astype(F32)
    return pl.pallas_call(
        functools.partial(_attn_prompt_kernel, tq=tq, tk=tk, lam0=lam0),
        out_shape=jax.ShapeDtypeStruct((b, s, w), F32),
        grid=(b, A_HEADS, s // tq),
        in_specs=[_const_spec(lam_vec.shape), _const_spec(bias.shape),
                  pl.BlockSpec((1, tq, LANES), blk),
                  pl.BlockSpec((1, s, LANES), lambda bi, hi, qi: (bi, 0, hi)),
                  pl.BlockSpec((nkb, LANES, tk), lambda bi, hi, qi: (bi, hi, 0))],
        out_specs=pl.BlockSpec((1, tq, LANES), blk),
        scratch_shapes=[pltpu.VMEM((2, tk, tq), F32), pltpu.VMEM((2, tk, tq), F32),
                        pltpu.VMEM((2, tk, tq), BF16), pltpu.VMEM((2, tk, tq), BF16),
                        pltpu.VMEM((2, LANES, tq), F32)],
        compiler_params=_cparams(("parallel", "parallel", "arbitrary")),
        name="attn_prompt",
    )(lam_vec, bias, qs, kb, vt)


def _attn_sample_kernel(lam_ref, q_ref, kc_ref, vc_ref, kn_ref, vn_ref, o_ref, *, lam0):
    q = q_ref[0]
    vc = vc_ref[0]
    vn = vn_ref[0]
    outs = []
    for kcm, knm in zip(_split_maps(kc_ref[0]), _split_maps(kn_ref[0])):
        sc = _dot_nt(q, kcm)
        sn = _dot_nt(q, knm)
        m = jnp.maximum(jnp.max(sc, axis=-1, keepdims=True), jnp.max(sn, axis=-1, keepdims=True))
        pc = jnp.exp2(sc - m)
        pn = jnp.exp2(sn - m)
        l = jnp.sum(pc, axis=-1, keepdims=True) + jnp.sum(pn, axis=-1, keepdims=True)
        outs.append((_dot(pc.astype(BF16), vc) + _dot(pn.astype(BF16), vn)) / l)
    o_ref[0] = (outs[0] - _lambda(lam_ref, lam0) * outs[1]).astype(o_ref.dtype)


def _attn_sample(lam_vec, qs, kc, vc, kn, vn, lam0):
    b, t, w = qs.shape
    past = kc.shape[1]
    blk = lambda bi, hi: (bi, 0, hi)
    return pl.pallas_call(
        functools.partial(_attn_sample_kernel, lam0=lam0),
        out_shape=jax.ShapeDtypeStruct((b, t, w), F32),
        grid=(b, A_HEADS),
        in_specs=[_const_spec(lam_vec.shape), pl.BlockSpec((1, t, LANES), blk),
                  pl.BlockSpec((1, past, LANES), blk), pl.BlockSpec((1, past, LANES), blk),
                  pl.BlockSpec((1, t, LANES), blk), pl.BlockSpec((1, t, LANES), blk)],
        out_specs=pl.BlockSpec((1, t, LANES), blk),
        compiler_params=_cparams(("parallel", "parallel")),
        name="attn_sample",
    )(lam_vec, qs, kc, vc, kn, vn)


def _cumsum_rows(x):
    c = x.shape[0]
    row = lax.broadcasted_iota(I32, x.shape, 0)
    d = 1
    while d < c:
        x = x + jnp.where(row >= d, pltpu.roll(x, d, axis=0), 0.0)
        d *= 2
    return x


def _block_ref_rows(cum, m):
    c, n = cum.shape
    if m >= 8:
        r = cum.reshape(c // (2 * m), 2 * m, n)[:, m - 1:m, :]
        return jnp.broadcast_to(r, (c // (2 * m), 2 * m, n)).reshape(c, n)
    pos = lax.broadcasted_iota(I32, cum.shape, 0) % (2 * m)
    out = cum
    for p in range(2 * m):
        shift = p - (m - 1)
        if shift != 0:
            out = jnp.where(pos == p, pltpu.roll(cum, shift % c, axis=0), out)
    return out


def _hgrn_kernel(hq_ref, hk_ref, hv_ref, lf_ref, s0_ref, o_ref, st_ref, stt_ref, *, c):
    ci = pl.program_id(1)

    @pl.when(ci == 0)
    def _():
        for h in range(B_HEADS):
            stt_ref[h] = s0_ref[0, h].T

    row = lax.broadcasted_iota(I32, (c, B_DIM), 0)
    r2 = lax.broadcasted_iota(I32, (c, c), 0)
    c2 = lax.broadcasted_iota(I32, (c, c), 1)
    levels = []
    m = c // 2
    while m >= 1:
        levels.append((m, (row % (2 * m)) >= m, (r2 // (2 * m)) == (c2 // (2 * m))))
        m //= 2
    for h in range(B_HEADS):
        sl = slice(h * B_DIM, (h + 1) * B_DIM)
        hq = hq_ref[0, :, sl].astype(F32)
        hk = hk_ref[0, :, sl].astype(F32)
        hv = hv_ref[0, :, sl]
        cum = _cumsum_rows(lf_ref[0, :, sl])
        last = cum[c - 1:c, :]

        a = None
        for m, upper, same_block in levels:
            d = cum - _block_ref_rows(cum, m)
            qd = hq * jnp.exp(jnp.where(upper, d, -jnp.inf))
            kd = hk * jnp.exp(jnp.where(upper, -jnp.inf, -d))
            p = _dot_nt(qd.astype(BF16), kd.astype(BF16))
            a = jnp.where(same_block, p, 0.0 if a is None else a)

        stt = stt_ref[h]
        o = _dot_nt((hq * jnp.exp(cum)).astype(BF16), stt.astype(BF16))
        o = o + _dot(a.astype(BF16), hv)
        o = o + jnp.sum(hq * hk, axis=-1, keepdims=True) * hv.astype(F32)
        o_ref[0, :, sl] = o.astype(o_ref.dtype)

        kl = (hk * jnp.exp(last - cum)).astype(BF16)
        stt_ref[h] = stt * jnp.exp(last) + _dot_tn(hv, kl)

    @pl.when(ci == pl.num_programs(1) - 1)
    def _():
        for h in range(B_HEADS):
            st_ref[0, h] = stt_ref[h].T


def _hgrn(hq, hk, hv, lf, s0, c):
    b, s, w = hq.shape
    blk = lambda bi, ci: (bi, ci, 0)
    sblk = lambda bi, ci: (bi, 0, 0, 0)
    return pl.pallas_call(
        functools.partial(_hgrn_kernel, c=c),
        out_shape=[jax.ShapeDtypeStruct((b, s, w), F32),
                   jax.ShapeDtypeStruct((b, B_HEADS, B_DIM, B_DIM), F32)],
        grid=(b, s // c),
        in_specs=[pl.BlockSpec((1, c, w), blk)] * 4 + [pl.BlockSpec((1, B_HEADS, B_DIM, B_DIM), sblk)],
        out_specs=[pl.BlockSpec((1, c, w), blk), pl.BlockSpec((1, B_HEADS, B_DIM, B_DIM), sblk)],
        scratch_shapes=[pltpu.VMEM((B_HEADS, B_DIM, B_DIM), F32)],
        compiler_params=_cparams(("parallel", "arbitrary")),
        name="hgrn",
    )(hq, hk, hv, lf, s0)


def _mem_kv_kernel(m_ref, g_ref, wk_ref, wv_ref, gk_ref, k_ref, v_ref):
    hm = _rms(m_ref[...], g_ref[...]).astype(BF16)
    k = _dot(hm, wk_ref[...])
    gk = gk_ref[...]
    hd = k.shape[-1] // X_HEADS
    for h in range(X_HEADS):
        sl = slice(h * hd, (h + 1) * hd)
        k_ref[:, sl] = _rms(k[:, sl], gk)
    v_ref[...] = _dot(hm, wv_ref[...])


def _mem_kv(mem2d, norm_g, w_mk_bf, w_mv_bf, xk_g, tm):
    t, d = mem2d.shape
    row = lambda i: (i, 0)
    return pl.pallas_call(
        _mem_kv_kernel,
        out_shape=[jax.ShapeDtypeStruct((t, d), F32)] * 2,
        grid=(t // tm,),
        in_specs=[pl.BlockSpec((tm, d), row), _const_spec((1, d)), _const_spec(w_mk_bf.shape),
                  _const_spec(w_mv_bf.shape), _const_spec((1, xk_g.shape[-1]))],
        out_specs=[pl.BlockSpec((tm, d), row)] * 2,
        compiler_params=_cparams(("parallel",)),
        name="mem_kv",
    )(mem2d, norm_g.reshape(1, d), w_mk_bf, w_mv_bf, xk_g.reshape(1, -1))


def _post_kernel(x_ref, oa_ref, ob_ref, gt_ref, sg_ref, gg_ref, wo_ref,
                 nx_ref, wq_ref, xq_ref, mk_ref, mv_ref, wxo_ref,
                 nf_ref, wr_ref, br_ref,
                 x2_ref, xs_ref, rloc_ref, gate_ref, cnt_ref, *, lam0):
    ya = []
    yb = []
    for h in range(A_HEADS):
        sl = slice(h * LANES, (h + 1) * LANES)
        ya.append((_rms(oa_ref[:, sl], sg_ref[...]) * (1.0 - lam0)).astype(BF16))
        yb.append((_rms(ob_ref[:, sl], gg_ref[...]) * gt_ref[:, sl].astype(F32)).astype(BF16))
    y = jnp.concatenate(ya + yb, axis=-1)
    x1 = x_ref[...] + _dot(y, wo_ref[...])

    q = _dot(_rms(x1, nx_ref[...]).astype(BF16), wq_ref[...])
    hd = q.shape[-1] // X_HEADS
    os_ = []
    for h in range(X_HEADS):
        sl = slice(h * hd, (h + 1) * hd)
        qh = _rms(q[:, sl], xq_ref[...]).astype(BF16)
        s = _dot_nt(qh, mk_ref[0, :, sl]) * (hd ** -0.5)
        p = jnp.exp(s - jnp.max(s, axis=-1, keepdims=True))
        p = p / jnp.sum(p, axis=-1, keepdims=True)
        os_.append(_dot(p.astype(BF16), mv_ref[0, :, sl]).astype(BF16))
    x2 = x1 + _dot(jnp.concatenate(os_, axis=-1), wxo_ref[...])
    x2_ref[...] = x2

    h2 = _rms(x2, nf_ref[...])
    w = h2.shape[0]
    h_hi = h2.astype(BF16)
    h_lo = (h2 - h_hi.astype(F32)).astype(BF16)
    wr = wr_ref[...]
    w_hi = wr.astype(BF16)
    w_lo = (wr - w_hi.astype(F32)).astype(BF16)
    a = _dot_nt(jnp.concatenate([w_hi, w_lo], axis=0), h_hi)
    logits = a[:N_EXPERTS] + a[N_EXPERTS:] + _dot_nt(w_hi, h_lo) + br_ref[...]
    erow = lax.broadcasted_iota(I32, logits.shape, 0)
    picks, vals = [], []
    for k in range(TOP_K):
        mx = jnp.max(logits, axis=0, keepdims=True)
        ix = jnp.min(jnp.where(logits == mx, erow, N_EXPERTS), axis=0, keepdims=True)
        pick = erow == ix
        picks.append(pick)
        vals.append(mx)
        logits = jnp.where(pick, -jnp.inf, logits)
    ex = [jnp.exp(v - vals[0]) for v in vals]
    den = ex[0] + ex[1] + ex[2] + ex[3]

    onehot = jnp.where(picks[0] | picks[1] | picks[2] | picks[3], 1.0, 0.0)
    earlier = (lax.broadcasted_iota(I32, (w, w), 0) < lax.broadcasted_iota(I32, (w, w), 1))
    rank = _dot(onehot.astype(BF16), jnp.where(earlier, 1.0, 0.0).astype(BF16))
    cnt = jnp.sum(onehot, axis=1, keepdims=True).astype(I32)
    pcnt = ((cnt + (ROW_BLOCK - 1)) // ROW_BLOCK) * ROW_BLOCK
    pc_b = jnp.broadcast_to(pcnt.astype(F32), onehot.shape)
    seg_off = _cumsum_rows(pc_b) - pc_b
    cap = xs_ref.shape[0]
    riota = lax.broadcasted_iota(I32, (cap, w), 0)
    sel = None
    lane8 = lax.broadcasted_iota(I32, rloc_ref.shape, 0)
    rloc_out = jnp.zeros(rloc_ref.shape, I32)
    gate_out = jnp.zeros(gate_ref.shape, F32)
    for k in range(TOP_K):
        rloc = jnp.sum(jnp.where(picks[k], seg_off + rank, 0.0), axis=0, keepdims=True).astype(I32)
        hit = riota == rloc
        sel = hit if sel is None else (sel | hit)
        rloc_out = jnp.where(lane8 == k, rloc, rloc_out)
        gate_out = jnp.where(lane8 == k, ex[k] / den, gate_out)
    xs_ref[...] = _dot(jnp.where(sel, 1.0, 0.0).astype(BF16), h_hi)
    rloc_ref[...] = rloc_out
    gate_ref[...] = gate_out
    cnt_ref[...] = jnp.broadcast_to(cnt, cnt_ref.shape)


def _window_cap(w):
    cap = w * TOP_K + N_EXPERTS * (ROW_BLOCK - 1) + 2 * ROW_BLOCK
    return -(-cap // LANES) * LANES


def _post(x2d, oa, ob, gt, subln_g, gn_g, w_out_bf, norm_x_g, w_xq_bf, xq_g, mk_bf, mv_bf,
          w_xo_bf, norm_ffn_g, w_router, b_router, lam0, tm, rows_per_batch):
    t, d = x2d.shape
    n_win = t // tm
    cap = _window_cap(tm)
    row = lambda i: (i, 0)
    tiles_per_batch = rows_per_batch // tm
    mem = lambda i: (i // tiles_per_batch, 0, 0)
    n_mem = mk_bf.shape[1]
    vec = lambda a: a.reshape(1, -1)
    return pl.pallas_call(
        functools.partial(_post_kernel, lam0=lam0),
        out_shape=[jax.ShapeDtypeStruct((t, d), F32),
                   jax.ShapeDtypeStruct((n_win * cap, d), F32),
                   jax.ShapeDtypeStruct((n_win * 8, tm), I32),
                   jax.ShapeDtypeStruct((n_win * 8, tm), F32),
                   jax.ShapeDtypeStruct((n_win * N_EXPERTS, LANES), I32)],
        grid=(n_win,),
        in_specs=[pl.BlockSpec((tm, d), row), pl.BlockSpec((tm, A_WIDTH), row),
                  pl.BlockSpec((tm, A_WIDTH), row), pl.BlockSpec((tm, A_WIDTH), row),
                  _const_spec((1, LANES)), _const_spec((1, LANES)), _const_spec(w_out_bf.shape),
                  _const_spec((1, d)), _const_spec(w_xq_bf.shape), _const_spec((1, xq_g.shape[-1])),
                  pl.BlockSpec((1, n_mem, d), mem), pl.BlockSpec((1, n_mem, d), mem),
                  _const_spec(w_xo_bf.shape),
                  _const_spec((1, d)), _const_spec((N_EXPERTS, d)), _const_spec((N_EXPERTS, 1))],
        out_specs=[pl.BlockSpec((tm, d), row), pl.BlockSpec((cap, d), row),
                   pl.BlockSpec((8, tm), row), pl.BlockSpec((8, tm), row),
                   pl.BlockSpec((N_EXPERTS, LANES), row)],
        compiler_params=_cparams(("parallel",)),
        name="post",
    )(x2d, oa, ob, gt, vec(subln_g), vec(gn_g), w_out_bf, vec(norm_x_g), w_xq_bf, vec(xq_g),
      mk_bf, mv_bf, w_xo_bf, vec(norm_ffn_g), w_router.T, b_router.reshape(N_EXPERTS, 1))


def _moe_expert_kernel(te_ref, tf_ref, tv_ref, g0_ref, gn_ref, sd_ref, xs_hbm,
                       wgu_ref, bgu_ref, wd_ref, bd_ref, ys_hbm,
                       xbuf, ybuf, wgu_bf, wd_bf, gsem, ssem, *, tm):
    i = pl.program_id(0)
    last = pl.num_programs(0) - 1
    slot = i % 2
    nb = tm // ROW_BLOCK
    valid = tv_ref[i] == 1
    next_valid = jnp.logical_and(i < last, tv_ref[jnp.minimum(i + 1, last)] == 1)

    def gather(tab_ref, s):
        for j in range(nb):
            r = pl.multiple_of(tab_ref[0, 0, j], ROW_BLOCK)
            pltpu.make_async_copy(xs_hbm.at[pl.ds(r, ROW_BLOCK)],
                                  xbuf.at[s, pl.ds(j * ROW_BLOCK, ROW_BLOCK)],
                                  gsem.at[s]).start(priority=j % 2)

    def scatter(s):
        for j in range(nb):
            r = pl.multiple_of(sd_ref[0, 0, j], ROW_BLOCK)
            pltpu.make_async_copy(ybuf.at[s, pl.ds(j * ROW_BLOCK, ROW_BLOCK)],
                                  ys_hbm.at[pl.ds(r, ROW_BLOCK)], ssem.at[s]).start(priority=j % 2)

    def wait_gather(s):
        pltpu.make_async_copy(xs_hbm.at[pl.ds(0, tm)], xbuf.at[s], gsem.at[s]).wait()

    def wait_scatter(s):
        pltpu.make_async_copy(ybuf.at[s], ys_hbm.at[pl.ds(0, tm)], ssem.at[s]).wait()

    @pl.when(jnp.logical_and(i == 0, valid))
    def _():
        gather(g0_ref, 0)

    @pl.when(jnp.logical_and(valid, tf_ref[i] == 1))
    def _():
        wgu_bf[...] = wgu_ref[0].astype(BF16)
        wd_bf[...] = wd_ref[0].astype(BF16)

    @pl.when(valid)
    def _():
        wait_gather(slot)

        @pl.when(next_valid)
        def _():
            gather(gn_ref, 1 - slot)

        de = wd_bf.shape[0]
        gu = _dot(xbuf[slot].astype(BF16), wgu_bf[...]) + bgu_ref[0]
        gate = jnp.minimum(gu[:, :de], SWIGLU_LIMIT)
        up = jnp.clip(gu[:, de:], -SWIGLU_LIMIT, SWIGLU_LIMIT)
        act = (up + 1.0) * gate * _sigmoid(SWIGLU_ALPHA * gate)
        ybuf[slot] = _dot(act.astype(BF16), wd_bf[...]) + bd_ref[0]

        @pl.when(i > 0)
        def _():
            wait_scatter(1 - slot)

        scatter(slot)

        @pl.when(jnp.logical_not(next_valid))
        def _():
            wait_scatter(slot)


def _moe_expert(tables, xs, w_gate_up, b_gate_up, w_down, b_down, tm):
    tile_expert, tile_first, tile_valid, gsrc, sdst = tables
    nt = tile_expert.shape[0]
    nb = tm // ROW_BLOCK
    d = xs.shape[1]
    ne, _, dgu = w_gate_up.shape
    de = w_down.shape[1]
    smem = lambda f: pl.BlockSpec((1, 1, nb), f, memory_space=pltpu.SMEM)
    wmap = lambda i, te, tf, tv: (te[i], 0, 0)
    grid_spec = pltpu.PrefetchScalarGridSpec(
        num_scalar_prefetch=3,
        grid=(nt,),
        in_specs=[smem(lambda i, te, tf, tv: (0, 0, 0)),
                  smem(lambda i, te, tf, tv: (jnp.minimum(i + 1, nt - 1), 0, 0)),
                  smem(lambda i, te, tf, tv: (i, 0, 0)),
                  pl.BlockSpec(memory_space=pl.ANY),
                  pl.BlockSpec((1, d, dgu), wmap), pl.BlockSpec((1, 1, dgu), wmap),
                  pl.BlockSpec((1, de, d), wmap), pl.BlockSpec((1, 1, d), wmap)],
        out_specs=pl.BlockSpec(memory_space=pl.ANY),
        scratch_shapes=[pltpu.VMEM((2, tm, d), F32), pltpu.VMEM((2, tm, d), F32),
                        pltpu.VMEM((d, dgu), BF16), pltpu.VMEM((de, d), BF16),
                        pltpu.SemaphoreType.DMA((2,)), pltpu.SemaphoreType.DMA((2,))],
    )
    return pl.pallas_call(
        functools.partial(_moe_expert_kernel, tm=tm),
        out_shape=jax.ShapeDtypeStruct(xs.shape, F32),
        grid_spec=grid_spec,
        input_output_aliases={6: 0},
        compiler_params=_cparams(("arbitrary",)),
        name="moe_expert",
    )(tile_expert, tile_first, tile_valid, gsrc.reshape(nt, 1, nb), gsrc.reshape(nt, 1, nb),
      sdst.reshape(nt, 1, nb), xs, w_gate_up, b_gate_up.reshape(ne, 1, dgu), w_down,
      b_down.reshape(ne, 1, d))


def _moe_combine_kernel(x_ref, ys_ref, rloc_ref, gate_ref, o_ref):
    cap = ys_ref.shape[0]
    w = x_ref.shape[0]
    riota = lax.broadcasted_iota(I32, (cap, w), 0)
    gt = jnp.zeros((cap, w), F32)
    for k in range(TOP_K):
        gt = jnp.where(riota == rloc_ref[k:k + 1, :], gate_ref[k:k + 1, :], gt)
    o_ref[...] = x_ref[...] + _dot_tn(gt.astype(BF16), ys_ref[...].astype(BF16))


def _moe_combine(x2, ys, rloc, gate, tm):
    t, d = x2.shape
    cap = _window_cap(tm)
    row = lambda i: (i, 0)
    return pl.pallas_call(
        _moe_combine_kernel,
        out_shape=jax.ShapeDtypeStruct((t, d), F32),
        grid=(t // tm,),
        in_specs=[pl.BlockSpec((tm, d), row), pl.BlockSpec((cap, d), row),
                  pl.BlockSpec((8, tm), row), pl.BlockSpec((8, tm), row)],
        out_specs=pl.BlockSpec((tm, d), row),
        compiler_params=_cparams(("parallel",)),
        name="moe_combine",
    )(x2, ys, rloc, gate)


def _moe_tables(cnt, cap, n_picks, tm_e):
    n_win = cnt.shape[0]
    bpt = tm_e // ROW_BLOCK
    nt = (n_picks // ROW_BLOCK + n_win * N_EXPERTS) // bpt + N_EXPERTS
    blk = (cnt + ROW_BLOCK - 1) // ROW_BLOCK
    seg_blk = jnp.cumsum(blk, axis=1) - blk
    cum_incl = jnp.cumsum(blk, axis=0)
    cum_excl = cum_incl - blk
    tot = cum_incl[-1]
    tiles = (tot + bpt - 1) // bpt
    tile_end = jnp.cumsum(tiles)
    tile_beg = tile_end - tiles
    ti = jnp.arange(nt, dtype=I32)
    te = jnp.minimum(jnp.sum((ti[:, None] >= tile_end[None, :]).astype(I32), axis=1), N_EXPERTS - 1)
    tv = ti < tile_end[-1]
    tf = jnp.concatenate([jnp.ones((1,), bool), te[1:] != te[:-1]])
    v = (ti - tile_beg[te])[:, None] * bpt + jnp.arange(bpt, dtype=I32)[None, :]
    ok = tv[:, None] & (v < tot[te][:, None])
    widx = jnp.sum((cum_incl.T[te][:, None, :] <= v[:, :, None]).astype(I32), axis=-1)
    onehot = jnp.minimum(widx, n_win - 1)[:, :, None] == jnp.arange(n_win, dtype=I32)
    base = (jnp.arange(n_win, dtype=I32) * cap)[None, :] + (seg_blk.T[te] - cum_excl.T[te]) * ROW_BLOCK
    row = jnp.sum(jnp.where(onehot, base[:, None, :], 0), axis=-1) + v * ROW_BLOCK
    assert n_win >= bpt
    spare = jnp.arange(bpt, dtype=I32)[None, :] * cap + (cap - 2 * ROW_BLOCK)
    gsrc = jnp.where(ok, row, cap - ROW_BLOCK).astype(I32)
    sdst = jnp.where(ok, row, spare).astype(I32)
    return te.astype(I32), tf.astype(I32), tv.astype(I32), gsrc, sdst


def _moe(x2, xs, rloc, gate, cnt, w_gate_up, b_gate_up, w_down, b_down, tm_w, tm_e):
    n_win = x2.shape[0] // tm_w
    cnt2 = cnt.reshape(n_win, N_EXPERTS, LANES)[:, :, 0]
    tables = _moe_tables(cnt2, _window_cap(tm_w), x2.shape[0] * TOP_K, tm_e)
    ys = _moe_expert(tables, xs, w_gate_up, b_gate_up, w_down, b_down, tm_e)
    return _moe_combine(x2, ys, rloc, gate, tm_w)


def _pick(n, pref):
    return pref if n % pref == 0 else n


def _group(x, mk_bf, mv_bf, attn_fn, attn_tk, s0, l, p, bf, lam0):
    b, t, d = x.shape
    x2d = x.reshape(b * t, d)
    (qs, kb, vb, k32, v32, hq, hk, hv, lf, gt) = _mixer_in(
        x2d, p["norm_mix_g"][l], bf["w_in"][l], p["a_qnorm_g"][l], p["a_knorm_g"][l],
        p["hgrn_lb_logits"], l, _pick(b * t, 512), attn_tk)
    r3 = lambda a: a.reshape(b, t, -1)
    oa = attn_fn(r3(qs), r3(kb), vb if attn_tk else r3(vb))
    ob, s_new = _hgrn(r3(hq), r3(hk), r3(hv), r3(lf), s0, _pick(t, 128))
    tm_w = _pick(t, 512)
    tm_e = 512 if b * t >= 4096 else 64
    x2, xs, rloc, gate, cnt = _post(
        x2d, oa.reshape(b * t, -1), ob.reshape(b * t, -1), gt, p["a_subln_g"][l], p["b_gnorm_g"][l],
        bf["w_out"][l], p["norm_x_g"][l], bf["w_xq"][l], p["xq_norm_g"][l], mk_bf, mv_bf,
        bf["w_xo"][l], p["norm_ffn_g"][l], p["w_router"][l], p["b_router"][l], lam0,
        tm_w, t)
    y = _moe(x2, xs, rloc, gate, cnt, p["w_gate_up"][l], p["b_gate_up"][l], p["w_down"][l],
             p["b_down"][l], tm_w, tm_e)
    return (y.reshape(b, t, d), k32.reshape(b, t, 2 * A_HEADS, A_HEAD_DIM),
            v32.reshape(b, t, A_HEADS, 2 * A_HEAD_DIM), s_new)


def kernel(x_prompt, x_sample, mem_prompt, cache_attn_k, cache_attn_v, state_hgrn, cache_mem_k, cache_mem_v, norm_mix_g, w_in, a_qnorm_g, a_knorm_g, lambda_q1, lambda_k1, lambda_q2, lambda_k2, a_subln_g, hgrn_lb_logits, b_gnorm_g, w_out, norm_x_g, norm_mem_g, w_xq, w_mk, w_mv, xq_norm_g, xk_norm_g, w_xo, norm_ffn_g, w_router, b_router, w_gate_up, b_gate_up, w_down, b_down):
    p = dict(norm_mix_g=norm_mix_g, a_qnorm_g=a_qnorm_g, a_knorm_g=a_knorm_g, a_subln_g=a_subln_g,
             hgrn_lb_logits=hgrn_lb_logits, b_gnorm_g=b_gnorm_g, norm_x_g=norm_x_g,
             xq_norm_g=xq_norm_g, norm_ffn_g=norm_ffn_g, w_router=w_router, b_router=b_router,
             w_gate_up=w_gate_up, b_gate_up=b_gate_up, w_down=w_down, b_down=b_down)
    bf = dict(w_in=w_in.astype(BF16), w_out=w_out.astype(BF16), w_xq=w_xq.astype(BF16),
              w_xo=w_xo.astype(BF16))
    depth = w_in.shape[0]
    b, s, d = x_prompt.shape
    db, ds, _ = x_sample.shape
    n_mem = mem_prompt.shape[1]
    xp, xs = x_prompt, x_sample
    outs = [[] for _ in range(8)]
    for l in range(depth):
        lam0 = 0.8 - 0.6 * math.exp(-0.3 * l)
        lam_vec = jnp.stack([lambda_q1[l], lambda_k1[l], lambda_q2[l], lambda_k2[l]])

        mk, mv = _mem_kv(mem_prompt.reshape(b * n_mem, d), norm_mem_g[l], w_mk[l].astype(BF16),
                         w_mv[l].astype(BF16), xk_norm_g[l], _pick(b * n_mem, 512))
        tk = 4 * LANES
        attn_p = lambda q, k, vt: _attn_prompt(lam_vec, q, k, vt, lam0, 2 * tk, tk)
        s0 = jnp.zeros((b, B_HEADS, B_DIM, B_DIM), F32)
        xp, kp, vp, sp = _group(xp, mk.reshape(b, n_mem, d).astype(BF16),
                                mv.reshape(b, n_mem, d).astype(BF16), attn_p, tk, s0, l, p, bf, lam0)

        kc = cache_attn_k[l].reshape(db, -1, A_WIDTH).astype(BF16)
        vc = cache_attn_v[l].reshape(db, -1, A_WIDTH).astype(BF16)
        attn_s = lambda q, k, v: _attn_sample(lam_vec, q, kc, vc, k, v, lam0)
        xs, ks, vs, ss = _group(xs, cache_mem_k[l].reshape(db, n_mem, d).astype(BF16),
                                cache_mem_v[l].reshape(db, n_mem, d).astype(BF16), attn_s, None,
                                state_hgrn[l], l, p, bf, lam0)
        for lst, val in zip(outs, (kp, vp, sp, mk.reshape(b, n_mem, X_HEADS, -1),
                                   mv.reshape(b, n_mem, X_HEADS, -1), ks, vs, ss)):
            lst.append(val)
    return (xp, xs) + tuple(jnp.stack(o) for o in outs)
```
